```python
import math
import jax, jax.numpy as jnp
from jax import lax
import numpy as np


D_MODEL = 1024
BATCH = 8
SEQ = 2048
DEPTH = 2

CHUNK = 64
N_MEM = 256
HEAD_DIM = 64
MIX_WIDTH = (D_MODEL * 3) // 4
N_MIX_HEADS = MIX_WIDTH // HEAD_DIM
MEM_WIDTH = D_MODEL - MIX_WIDTH
N_MEM_HEADS = MEM_WIDTH // HEAD_DIM
D_FF = 2816
DECAY_LORA = 64
AAA_LORA = 64
GATE_LORA = 128
Q_BLOCK = 128
N_MIXERS = 2
N_RWKV = (DEPTH + 1) // 2
N_SB = DEPTH // 2
ALPHA = (2 * DEPTH) ** 0.25
BETA = (8 * DEPTH) ** -0.25
LN_EPS = 1e-5
LNX_EPS = 64e-5
RWKV_SHIFT = 3 * MIX_WIDTH + DECAY_LORA + AAA_LORA + GATE_LORA
RWKV_IN = RWKV_SHIFT + MEM_WIDTH
SB_IN = 3 * MIX_WIDTH + MEM_WIDTH

kernel_name = "hybrid_rwkv7_stickbreak_macaron_deepnorm"


def layer_norm(x, g, b):
    xf = x.astype(jnp.float32)
    mu = jnp.mean(xf, axis=-1, keepdims=True)
    var = jnp.mean(jnp.square(xf - mu), axis=-1, keepdims=True)
    return ((xf - mu) * lax.rsqrt(var + LN_EPS) * g + b).astype(x.dtype)


def swiglu(x, w_gate, w_up, w_down):
    return (jax.nn.silu(x @ w_gate) * (x @ w_up)) @ w_down


def split_heads(t):
    b, s, _ = t.shape
    return t.reshape(b, s, -1, HEAD_DIM).transpose(0, 2, 1, 3)


def merge_heads(t):
    b, h, s, d = t.shape
    return t.transpose(0, 2, 1, 3).reshape(b, s, h * d)


def token_shift(p, mu):
    prev = jnp.pad(p, ((0, 0), (1, 0), (0, 0)))[:, :-1]
    return p + (prev - p) * mu


def memory_attention(mq, mem_k, mem_v):
    q = split_heads(mq)
    scores = jnp.einsum('bhsd,bhmd->bhsm', q, mem_k).astype(jnp.float32) / math.sqrt(HEAD_DIM)
    p = jax.nn.softmax(scores, axis=-1).astype(mq.dtype)
    return merge_heads(jnp.einsum('bhsm,bhmd->bhsd', p, mem_v))


def rwkv7_mix(z_in, mu, w0, w_up, a0, a_up, g_up, k_k, k_a, r_k, lnx_g, lnx_b):
    bsz, seq, _ = z_in.shape
    dt = z_in.dtype
    z = token_shift(z_in, mu)
    c = MIX_WIDTH
    r = z[..., :c]
    k = z[..., c:2 * c]
    v = z[..., 2 * c:3 * c]
    o = 3 * c
    dw = z[..., o:o + DECAY_LORA]
    o += DECAY_LORA
    da = z[..., o:o + AAA_LORA]
    o += AAA_LORA
    dg = z[..., o:o + GATE_LORA]
    w = -jax.nn.softplus(-(w0 + jnp.tanh(dw) @ w_up)) - 0.5
    a = jax.nn.sigmoid(a0 + da @ a_up)
    g = jax.nn.sigmoid(dg) @ g_up

    def hs(t):
        return t.astype(jnp.float32).reshape(bsz, seq, N_MIX_HEADS, HEAD_DIM)

    kk = hs(k * k_k)
    kk = kk * lax.rsqrt(jnp.maximum(jnp.sum(kk * kk, axis=-1, keepdims=True), 1e-24))
    k = hs(k * (1.0 + (a - 1.0) * k_a))
    r, v, a = hs(r), hs(v), hs(a)
    decay = jnp.exp(-jnp.exp(hs(w)))

    n_chunks = seq // CHUNK

    def to_chunks(t):
        return t.transpose(1, 0, 2, 3).reshape(n_chunks, CHUNK, bsz, N_MIX_HEADS, HEAD_DIM)

    xs = (to_chunks(r), to_chunks(decay), to_chunks(k), to_chunks(v), to_chunks(kk), to_chunks(kk * a))

    def step(state, inp):
        r_t, w_t, k_t, v_t, kk_t, b_t = inp
        sa = jnp.einsum('bhij,bhj->bhi', state, kk_t)
        state = (state * w_t[:, :, None, :] - sa[..., :, None] * b_t[..., None, :]
                 + v_t[..., :, None] * k_t[..., None, :])
        return state, jnp.einsum('bhij,bhj->bhi', state, r_t)

    def chunk_step(state, chunk_inp):
        return lax.scan(step, state, chunk_inp)

    state0 = jnp.zeros((bsz, N_MIX_HEADS, HEAD_DIM, HEAD_DIM), jnp.float32)
    _, out = lax.scan(chunk_step, state0, xs)
    out = out.reshape(seq, bsz, N_MIX_HEADS, HEAD_DIM).transpose(1, 0, 2, 3)
    m = jnp.mean(out, axis=-1, keepdims=True)
    var = jnp.mean(jnp.square(out - m), axis=-1, keepdims=True)
    out = ((out - m) * lax.rsqrt(var + LNX_EPS)).reshape(bsz, seq, c) * lnx_g + lnx_b
    bonus = jnp.sum(r * k * r_k, axis=-1, keepdims=True) * v
    y = (out + bonus.reshape(bsz, seq, c)) * g
    return y.astype(dt)


def stick_breaking_mix(q, k, v):
    q, k, v = split_heads(q), split_heads(k), split_heads(v)
    seq = q.shape[2]
    scale = 1.0 / math.sqrt(HEAD_DIM)
    outs = []
    for blk in range(seq // Q_BLOCK):
        start = blk * Q_BLOCK
        end = start + Q_BLOCK
        kb = k[:, :, :end]
        vb = v[:, :, :end]
        z = jnp.einsum('bhtd,bhsd->bhts', q[:, :, start:end], kb).astype(jnp.float32) * scale
        t_pos = start + jnp.arange(Q_BLOCK)[:, None]
        s_pos = jnp.arange(end)[None, :]
        before = s_pos < t_pos
        log_keep = jnp.where(before, -jax.nn.softplus(z), 0.0)
        later = lax.cumsum(log_keep, axis=3, reverse=True) - log_keep
        att = jnp.where(before, jnp.exp(jax.nn.log_sigmoid(z) + later), 0.0)
        outs.append(jnp.einsum('bhts,bhsd->bhtd', att.astype(v.dtype), vb))
    return merge_heads(jnp.concatenate(outs, axis=2))


def setup_inputs(seed: int = 0) -> dict:
    key = jax.random.key(seed)
    ks = jax.random.split(key, 32)
    f32 = jnp.float32
    d, f, c = D_MODEL, D_FF, MIX_WIDTH

    def nrm(k, shape, scale):
        return jax.random.normal(k, shape, f32) * scale

    return {
        "x": nrm(ks[0], (BATCH, SEQ, d), 1.0),
        "mem": nrm(ks[1], (BATCH, N_MEM, d), 1.0),
        "ffn1_w_gate": nrm(ks[2], (DEPTH, d, f), d ** -0.5),
        "ffn1_w_up": nrm(ks[3], (DEPTH, d, f), d ** -0.5),
        "ffn1_w_down": nrm(ks[4], (DEPTH, f, d), BETA * f ** -0.5),
        "ffn2_w_gate": nrm(ks[5], (DEPTH, d, f), d ** -0.5),
        "ffn2_w_up": nrm(ks[6], (DEPTH, d, f), d ** -0.5),
        "ffn2_w_down": nrm(ks[7], (DEPTH, f, d), BETA * f ** -0.5),
        "ln_g": 1.0 + nrm(ks[8], (DEPTH, 3, d), 0.02),
        "ln_b": nrm(ks[9], (DEPTH, 3, d), 0.02),
        "w_out": nrm(ks[10], (DEPTH, d, d), BETA * d ** -0.5),
        "w_mem_kv": nrm(ks[11], (d, 2 * MEM_WIDTH), d ** -0.5),
        "rwkv_w_in": nrm(ks[12], (N_RWKV, d, RWKV_IN), d ** -0.5),
        "rwkv_mu": jax.random.uniform(ks[13], (N_RWKV, RWKV_SHIFT), f32),
        "rwkv_w0": jax.random.uniform(ks[14], (N_RWKV, c), f32, -6.0, -1.0),
        "rwkv_w_up": nrm(ks[15], (N_RWKV, DECAY_LORA, c), 0.1 * DECAY_LORA ** -0.5),
        "rwkv_a0": nrm(ks[16], (N_RWKV, c), 0.5),
        "rwkv_a_up": nrm(ks[17], (N_RWKV, AAA_LORA, c), 0.5 * AAA_LORA ** -0.5),
        "rwkv_g_up": nrm(ks[18], (N_RWKV, GATE_LORA, c), GATE_LORA ** -0.5),
        "rwkv_k_k": 1.0 + nrm(ks[19], (N_RWKV, c), 0.1),
        "rwkv_k_a": 1.0 + nrm(ks[20], (N_RWKV, c), 0.1),
        "rwkv_r_k": nrm(ks[21], (N_RWKV, N_MIX_HEADS, HEAD_DIM), 0.1),
        "rwkv_lnx_g": 1.0 + nrm(ks[22], (N_RWKV, c), 0.02),
        "rwkv_lnx_b": nrm(ks[23], (N_RWKV, c), 0.02),
        "sb_w_in": nrm(ks[24], (N_SB, d, SB_IN), d ** -0.5),
    }


def reference(x, mem, ffn1_w_gate, ffn1_w_up, ffn1_w_down, ffn2_w_gate, ffn2_w_up, ffn2_w_down,
              ln_g, ln_b, w_out, w_mem_kv, rwkv_w_in, rwkv_mu, rwkv_w0, rwkv_w_up, rwkv_a0,
              rwkv_a_up, rwkv_g_up, rwkv_k_k, rwkv_k_a, rwkv_r_k, rwkv_lnx_g, rwkv_lnx_b, sb_w_in):
    mem_kv = mem @ w_mem_kv
    mem_k = split_heads(mem_kv[..., :MEM_WIDTH])
    mem_v = split_heads(mem_kv[..., MEM_WIDTH:])
    c = MIX_WIDTH
    for i in range(DEPTH):
        x = layer_norm(ALPHA * x + 0.5 * swiglu(x, ffn1_w_gate[i], ffn1_w_up[i], ffn1_w_down[i]),
                       ln_g[i, 0], ln_b[i, 0])
        j = i // N_MIXERS
        if i % N_MIXERS == 0:
            proj = x @ rwkv_w_in[j]
            mix = rwkv7_mix(proj[..., :RWKV_SHIFT], rwkv_mu[j], rwkv_w0[j], rwkv_w_up[j],
                            rwkv_a0[j], rwkv_a_up[j], rwkv_g_up[j], rwkv_k_k[j], rwkv_k_a[j],
                            rwkv_r_k[j], rwkv_lnx_g[j], rwkv_lnx_b[j])
            mq = proj[..., RWKV_SHIFT:]
        else:
            proj = x @ sb_w_in[j]
            mix = stick_breaking_mix(proj[..., :c], proj[..., c:2 * c], proj[..., 2 * c:3 * c])
            mq = proj[..., 3 * c:]
        heads = jnp.concatenate([mix, memory_attention(mq, mem_k, mem_v)], axis=-1)
        x = layer_norm(ALPHA * x + heads @ w_out[i], ln_g[i, 1], ln_b[i, 1])
        x = layer_norm(ALPHA * x + 0.5 * swiglu(x, ffn2_w_gate[i], ffn2_w_up[i], ffn2_w_down[i]),
                       ln_g[i, 2], ln_b[i, 2])
    return x
```

```python
import functools
import math

import jax
import jax.numpy as jnp
from jax import lax
from jax.experimental import pallas as pl
from jax.experimental.pallas import tpu as pltpu

F32 = jnp.float32
BF16 = jnp.bfloat16

HEAD_DIM = 64
N_MEM_HEADS = 4
MEM_WIDTH = N_MEM_HEADS * HEAD_DIM
DECAY_LORA = 64
AAA_LORA = 64
GATE_LORA = 128
LORA_WIDTH = DECAY_LORA + AAA_LORA + GATE_LORA
LN_EPS = 1e-5
LNX_EPS = 64e-5
RWKV_CHUNK = 64
INV_BLOCK = 16
SB_BLOCK = 128
VMEM_LIMIT = 48 * 1024 * 1024


def _cparams(*sem):
    return pltpu.CompilerParams(dimension_semantics=sem, vmem_limit_bytes=VMEM_LIMIT)


def _dot(a, b):
    return jnp.dot(a, b, preferred_element_type=F32)


def _dot_nt(a, b):
    return lax.dot_general(a, b, (((1,), (1,)), ((), ())), preferred_element_type=F32)


def _dot_tn(a, b):
    return lax.dot_general(a, b, (((0,), (0,)), ((), ())), preferred_element_type=F32)


def _split_dot(x, ones, terms):
    acc = None
    rem = x
    for _ in range(terms):
        piece = rem.astype(BF16)
        part = _dot(piece, ones)
        acc = part if acc is None else acc + part
        rem = rem - piece.astype(F32)
    return acc


def _softplus(x):
    return jnp.maximum(x, 0.0) + jnp.log(1.0 + jnp.exp(-jnp.abs(x)))


def _layer_norm(y, g, b):
    mu = jnp.mean(y, axis=-1, keepdims=True)
    d = y - mu
    var = jnp.mean(d * d, axis=-1, keepdims=True)
    return d * lax.rsqrt(var + LN_EPS) * g + b


def _ffn_kernel(x_ref, wg_ref, wu_ref, wd_ref, g_ref, b_ref, o_ref, xb_ref, acc_ref, *, alpha):
    j = pl.program_id(1)

    @pl.when(j == 0)
    def _():
        xb_ref[...] = x_ref[...].astype(BF16)
        acc_ref[...] = jnp.zeros_like(acc_ref)

    xb = xb_ref[...]
    gate = _dot(xb, wg_ref[...])
    up = _dot(xb, wu_ref[...])
    h = gate * jax.nn.sigmoid(gate) * up
    acc_ref[...] += _dot(h.astype(BF16), wd_ref[...])

    @pl.when(j == pl.num_programs(1) - 1)
    def _():
        y = alpha * x_ref[...] + 0.5 * acc_ref[...]
        o_ref[...] = _layer_norm(y, g_ref[...], b_ref[...])


def _ffn_ln(x, wg, wu, wd, g, b, *, alpha, tm, tf):
    t, d = x.shape
    f = wg.shape[1]
    return pl.pallas_call(
        functools.partial(_ffn_kernel, alpha=alpha),
        grid=(t // tm, f // tf),
        in_specs=[
            pl.BlockSpec((tm, d), lambda i, j: (i, 0)),
            pl.BlockSpec((d, tf), lambda i, j: (0, j)),
            pl.BlockSpec((d, tf), lambda i, j: (0, j)),
            pl.BlockSpec((tf, d), lambda i, j: (j, 0)),
            pl.BlockSpec((1, d), lambda i, j: (0, 0)),
            pl.BlockSpec((1, d), lambda i, j: (0, 0)),
        ],
        out_specs=pl.BlockSpec((tm, d), lambda i, j: (i, 0)),
        out_shape=jax.ShapeDtypeStruct((t, d), F32),
        scratch_shapes=[pltpu.VMEM((tm, d), BF16), pltpu.VMEM((tm, d), F32)],
        compiler_params=_cparams("parallel", "arbitrary"),
        name="ffn_ln",
    )(x, wg, wu, wd, g, b)


def _proj_kernel(x_ref, w_ref, o_ref):
    o_ref[...] = _dot(x_ref[...].astype(BF16), w_ref[...]).astype(o_ref.dtype)


def _proj(x, w, *, tm, tn, out_dtype):
    t, d = x.shape
    n = w.shape[1]
    return pl.pallas_call(
        _proj_kernel,
        grid=(t // tm, n // tn),
        in_specs=[pl.BlockSpec((tm, d), lambda i, j: (i, 0)),
                  pl.BlockSpec((d, tn), lambda i, j: (0, j))],
        out_specs=pl.BlockSpec((tm, tn), lambda i, j: (i, j)),
        out_shape=jax.ShapeDtypeStruct((t, n), out_dtype),
        compiler_params=_cparams("parallel", "parallel"),
        name="proj",
    )(x, w)


def _rwkv_prep_kernel(x_ref, win_ref, mu_ref, lora_ref, w0_ref, a0_ref, kkw_ref, kaw_ref, rkw_ref, bd_ref,
                      r_o, lw_o, k_o, v_o, kk_o, b_o, g_o, bonus_o, mq_o, carry_ref, *, tiles_per_seq, c):
    i = pl.program_id(0)
    shift = 3 * c + LORA_WIDTH

    @pl.when(i % tiles_per_seq == 0)
    def _():
        carry_ref[...] = jnp.zeros_like(carry_ref)

    p = _dot(x_ref[...].astype(BF16), win_ref[...])
    tm = p.shape[0]
    ps = p[:, :shift]
    mq_o[...] = p[:, shift:].astype(mq_o.dtype)

    row = lax.broadcasted_iota(jnp.int32, ps.shape, 0)
    prev = jnp.where(row == 0, carry_ref[...], pltpu.roll(ps, 1, 0))
    carry_ref[...] = ps[tm - 1:tm, :]
    z = ps + (prev - ps) * mu_ref[...]

    r = z[:, :c]
    kraw = z[:, c:2 * c]
    v = z[:, 2 * c:3 * c]
    lo = z[:, 3 * c:]
    lane = lax.broadcasted_iota(jnp.int32, lo.shape, 1)
    lhs = jnp.where(lane < DECAY_LORA, jnp.tanh(lo),
                    jnp.where(lane < DECAY_LORA + AAA_LORA, lo, jax.nn.sigmoid(lo)))
    pre = _dot(lhs.astype(BF16), lora_ref[...])
    w = -_softplus(-(pre[:, :c] + w0_ref[...])) - 0.5
    a = jax.nn.sigmoid(pre[:, c:2 * c] + a0_ref[...])
    g_o[...] = pre[:, 2 * c:]

    bd = bd_ref[...]
    kkv = kraw * kkw_ref[...]
    ss = _split_dot(kkv * kkv, bd, 2)
    kkn = kkv * lax.rsqrt(jnp.maximum(ss, 1e-24))
    k2 = kraw * (1.0 + (a - 1.0) * kaw_ref[...])

    r_o[...] = r
    lw_o[...] = -jnp.exp(w)
    k_o[...] = k2
    v_o[...] = v
    kk_o[...] = kkn
    b_o[...] = kkn * a
    bonus_o[...] = _split_dot(r * k2 * rkw_ref[...], bd, 2) * v


def _rwkv_prep(x, w_in, mu, lora, w0, a0, k_k, k_a, r_k, bd, *, seq, tm):
    t, d = x.shape
    c = w0.shape[1]
    n_in = w_in.shape[1]
    shift = 3 * c + LORA_WIDTH
    row = lambda i: (i, 0)
    const = lambda i: (0, 0)
    wide = jax.ShapeDtypeStruct((t, c), F32)
    return pl.pallas_call(
        functools.partial(_rwkv_prep_kernel, tiles_per_seq=seq // tm, c=c),
        grid=(t // tm,),
        in_specs=[
            pl.BlockSpec((tm, d), row),
            pl.BlockSpec((d, n_in), const),
            pl.BlockSpec((1, shift), const),
            pl.BlockSpec((LORA_WIDTH, 3 * c), const),
            pl.BlockSpec((1, c), const), pl.BlockSpec((1, c), const), pl.BlockSpec((1, c), const),
            pl.BlockSpec((1, c), const), pl.BlockSpec((1, c), const),
            pl.BlockSpec((c, c), const),
        ],
        out_specs=[pl.BlockSpec((tm, c), row)] * 8 + [pl.BlockSpec((tm, MEM_WIDTH), row)],
        out_shape=[wide] * 8 + [jax.ShapeDtypeStruct((t, MEM_WIDTH), BF16)],
        scratch_shapes=[pltpu.VMEM((1, shift), F32)],
        compiler_params=_cparams("arbitrary"),
        name="rwkv_prep",
    )(x, w_in, mu, lora, w0, a0, k_k, k_a, r_k, bd)


def _unit_lower_inverse(m, eye, same_block):
    md = jnp.where(same_block, m, 0.0)
    n = m - md
    m2 = _dot(md, md)
    m4 = _dot(m2, m2)
    m8 = _dot(m4, m4)
    di = eye - md
    di = di + _dot(di, m2)
    di = di + _dot(di, m4)
    di = di + _dot(di, m8)
    xn = _dot(di, n)
    y = eye - xn
    y = y + _dot(y, _dot(xn, xn))
    return _dot(y, di)


def _rwkv_chunk_kernel(r_ref, lw_ref, k_ref, v_ref, kk_ref, b_ref, o_ref, s_ref, *, n_heads):
    @pl.when(pl.program_id(1) == 0)
    def _():
        s_ref[...] = jnp.zeros_like(s_ref)

    lw = lw_ref[0]
    L = lw.shape[0]
    ti = lax.broadcasted_iota(jnp.int32, (L, L), 0)
    si = lax.broadcasted_iota(jnp.int32, (L, L), 1)
    incl = si <= ti
    strict = si < ti
    eye = (si == ti).astype(F32)
    same_block = (si // INV_BLOCK) == (ti // INV_BLOCK)

    cum = _split_dot_left(incl.astype(BF16), lw, 3)
    cum_last = cum[L - 1:L, :]
    g_incl = jnp.exp(cum)
    g_excl = jnp.exp(cum - lw)
    g_inv = jnp.exp(-cum)
    g_rel = jnp.exp(cum_last - cum)
    g_last = jnp.exp(cum_last)

    kk = kk_ref[0]
    b = b_ref[0]
    k = k_ref[0]
    kap = kk * g_excl
    rho = r_ref[0] * g_incl
    bet = b * g_inv
    kt = k * g_inv
    bg = b * g_rel
    kg = k * g_rel
    v = v_ref[0]

    for h in range(n_heads):
        sl = slice(h * HEAD_DIM, (h + 1) * HEAD_DIM)
        kap_h, rho_h, bet_h, kt_h, v_h = kap[:, sl], rho[:, sl], bet[:, sl], kt[:, sl], v[:, sl]
        m_ab = jnp.where(strict, _dot_nt(kap_h, bet_h), 0.0)
        m_ak = jnp.where(strict, _dot_nt(kap_h, kt_h), 0.0)
        a_rb = jnp.where(incl, _dot_nt(rho_h, bet_h), 0.0)
        a_rk = jnp.where(incl, _dot_nt(rho_h, kt_h), 0.0)
        t_inv = _unit_lower_inverse(m_ab, eye, same_block)
        s = s_ref[h]
        u = -_dot(t_inv, _dot_nt(kap_h, s) + _dot(m_ak, v_h))
        o_ref[0, :, sl] = _dot_nt(rho_h, s) + _dot(a_rb, u) + _dot(a_rk, v_h)
        s_ref[h] = s * g_last[:, sl] + _dot_tn(u, bg[:, sl]) + _dot_tn(v_h, kg[:, sl])


def _split_dot_left(ones, x, terms):
    acc = None
    rem = x
    for _ in range(terms):
        piece = rem.astype(BF16)
        part = _dot(ones, piece)
        acc = part if acc is None else acc + part
        rem = rem - piece.astype(F32)
    return acc


def _rwkv_chunk(r, lw, k, v, kk, b):
    bsz, seq, c = r.shape
    n_heads = c // HEAD_DIM
    blk = pl.BlockSpec((1, RWKV_CHUNK, c), lambda bi, ci: (bi, ci, 0))
    return pl.pallas_call(
        functools.partial(_rwkv_chunk_kernel, n_heads=n_heads),
        grid=(bsz, seq // RWKV_CHUNK),
        in_specs=[blk] * 6,
        out_specs=blk,
        out_shape=jax.ShapeDtypeStruct((bsz, seq, c), F32),
        scratch_shapes=[pltpu.VMEM((n_heads, HEAD_DIM, HEAD_DIM), F32)],
        compiler_params=_cparams("parallel", "arbitrary"),
        name="rwkv_chunk",
    )(r, lw, k, v, kk, b)


def _sb_kernel(q_ref, k_ref, v_ref, o_ref):
    qi = pl.program_id(2)
    blk = q_ref.shape[0]
    scale = 1.0 / math.sqrt(HEAD_DIM)
    ti = lax.broadcasted_iota(jnp.int32, (blk, blk), 0)
    si = lax.broadcasted_iota(jnp.int32, (blk, blk), 1)
    before = si < ti
    upper = (ti > si).astype(BF16)
    q2 = q_ref[...]
    outs = []
    for hh in range(q2.shape[1] // HEAD_DIM):
        sl = slice(hh * HEAD_DIM, (hh + 1) * HEAD_DIM)
        q = q2[:, sl]

        def tile(j, carry, acc, diagonal):
            rows = pl.ds(pl.multiple_of(j * blk, blk), blk)
            z = _dot_nt(q, k_ref[rows, sl]) * scale
            sp = _softplus(z)
            log_keep = -sp
            if diagonal:
                log_keep = jnp.where(before, log_keep, 0.0)
            later = _split_dot(log_keep, upper, 2) + carry
            att = jnp.exp(z - sp + later)
            if diagonal:
                att = jnp.where(before, att, 0.0)
            acc = acc + _dot(att.astype(BF16), v_ref[rows, sl])
            return carry + jnp.sum(log_keep, axis=1, keepdims=True), acc

        carry, acc = tile(qi, jnp.zeros((blk, 1), F32), jnp.zeros((blk, HEAD_DIM), F32), True)

        def body(jj, state):
            return tile(qi - 1 - jj, state[0], state[1], False)

        _, acc = lax.fori_loop(0, qi, body, (carry, acc))
        outs.append(acc)
    o_ref[...] = jnp.concatenate(outs, axis=1).astype(o_ref.dtype)


def _sb_attention(p, *, bsz, seq, c):
    pair = 2 * HEAD_DIM
    n_pairs = c // pair
    nq = seq // SB_BLOCK
    return pl.pallas_call(
        _sb_kernel,
        grid=(bsz, n_pairs, nq),
        in_specs=[
            pl.BlockSpec((SB_BLOCK, pair), lambda b, h, i: (b * nq + i, h)),
            pl.BlockSpec((seq, pair), lambda b, h, i: (b, n_pairs + h)),
            pl.BlockSpec((seq, pair), lambda b, h, i: (b, 2 * n_pairs + h)),
        ],
        out_specs=pl.BlockSpec((SB_BLOCK, pair), lambda b, h, i: (b * nq + i, h)),
        out_shape=jax.ShapeDtypeStruct((bsz * seq, c), BF16),
        compiler_params=_cparams("parallel", "parallel", "arbitrary"),
        name="sb_attention",
    )(p, p, p)


def _memory_attention(mq, mem_kv):
    scale = 1.0 / math.sqrt(HEAD_DIM)
    outs = []
    for h in range(N_MEM_HEADS):
        sl = slice(h * HEAD_DIM, (h + 1) * HEAD_DIM)
        vl = slice(MEM_WIDTH + h * HEAD_DIM, MEM_WIDTH + (h + 1) * HEAD_DIM)
        s = _dot_nt(mq[:, sl], mem_kv[:, sl]) * scale
        e = jnp.exp(s - jnp.max(s, axis=-1, keepdims=True))
        prob = e / jnp.sum(e, axis=-1, keepdims=True)
        outs.append(_dot(prob.astype(BF16), mem_kv[:, vl]))
    return jnp.concatenate(outs, axis=1)


def _out_ln(mix_bf16, mq_ref, memkv_ref, x_ref, wout_ref, g_ref, b_ref, o_ref, *, alpha):
    c = mix_bf16.shape[1]
    mem_o = _memory_attention(mq_ref[...], memkv_ref[0])
    y = _dot(mix_bf16, wout_ref[:c, :]) + _dot(mem_o.astype(BF16), wout_ref[c:, :])
    o_ref[...] = _layer_norm(alpha * x_ref[...] + y, g_ref[...], b_ref[...])


def _sb_out_kernel(mix_ref, mq_ref, memkv_ref, x_ref, wout_ref, g_ref, b_ref, o_ref, *, alpha):
    _out_ln(mix_ref[...], mq_ref, memkv_ref, x_ref, wout_ref, g_ref, b_ref, o_ref, alpha=alpha)


def _rwkv_out_kernel(raw_ref, bonus_ref, gate_ref, lg_ref, lb_ref, bd_ref,
                     mq_ref, memkv_ref, x_ref, wout_ref, g_ref, b_ref, o_ref, *, alpha):
    raw = raw_ref[...]
    bd = bd_ref[...]
    inv_n = 1.0 / HEAD_DIM
    m = _split_dot(raw, bd, 3) * inv_n
    d = raw - m
    var = _split_dot(d * d, bd, 2) * inv_n
    normed = d * lax.rsqrt(var + LNX_EPS) * lg_ref[...] + lb_ref[...]
    mix = (normed + bonus_ref[...]) * gate_ref[...]
    _out_ln(mix.astype(BF16), mq_ref, memkv_ref, x_ref, wout_ref, g_ref, b_ref, o_ref, alpha=alpha)


def _mix_out_ln(mix_inputs, mq, mq_col, mem_kv, x, w_out, g, b, *, alpha, seq, tm, rwkv):
    t, d = x.shape
    tiles_per_seq = seq // tm
    row = lambda i: (i, 0)
    const = lambda i: (0, 0)
    c = d - MEM_WIDTH
    if rwkv:
        kern = _rwkv_out_kernel
        mix_specs = [pl.BlockSpec((tm, c), row)] * 3 + [pl.BlockSpec((1, c), const)] * 2 + [pl.BlockSpec((c, c), const)]
    else:
        kern = _sb_out_kernel
        mix_specs = [pl.BlockSpec((tm, c), row)]
    return pl.pallas_call(
        functools.partial(kern, alpha=alpha),
        grid=(t // tm,),
        in_specs=mix_specs + [
            pl.BlockSpec((tm, MEM_WIDTH), lambda i: (i, mq_col)),
            pl.BlockSpec((1,) + mem_kv.shape[1:], lambda i: (i // tiles_per_seq, 0, 0)),
            pl.BlockSpec((tm, d), row),
            pl.BlockSpec((d, d), const),
            pl.BlockSpec((1, d), const),
            pl.BlockSpec((1, d), const),
        ],
        out_specs=pl.BlockSpec((tm, d), row),
        out_shape=jax.ShapeDtypeStruct((t, d), F32),
        compiler_params=_cparams("parallel"),
        name="rwkv_out_ln" if rwkv else "sb_out_ln",
    )(*mix_inputs, mq, mem_kv, x, w_out, g, b)


def kernel(x, mem, ffn1_w_gate, ffn1_w_up, ffn1_w_down, ffn2_w_gate, ffn2_w_up, ffn2_w_down, ln_g, ln_b, w_out, w_mem_kv, rwkv_w_in, rwkv_mu, rwkv_w0, rwkv_w_up, rwkv_a0, rwkv_a_up, rwkv_g_up, rwkv_k_k, rwkv_k_a, rwkv_r_k, rwkv_lnx_g, rwkv_lnx_b, sb_w_in):
    bsz, seq, d = x.shape
    depth = ln_g.shape[0]
    c = d - MEM_WIDTH
    alpha = (2 * depth) ** 0.25
    n_mem = mem.shape[1]
    tm = min(512, seq)
    tf = 256

    head_of = jnp.arange(c) // HEAD_DIM
    bd = (head_of[:, None] == head_of[None, :]).astype(BF16)

    mem_kv = _proj(mem.reshape(bsz * n_mem, d), w_mem_kv.astype(BF16), tm=256, tn=2 * MEM_WIDTH, out_dtype=BF16)
    mem_kv = mem_kv.reshape(bsz, n_mem, 2 * MEM_WIDTH)

    def ffn(xf, wg, wu, wd, g, b):
        return _ffn_ln(xf, wg.astype(BF16), wu.astype(BF16), wd.astype(BF16), g[None], b[None],
                       alpha=alpha, tm=tm, tf=tf)

    xf = x.reshape(bsz * seq, d)
    for i in range(depth):
        xf = ffn(xf, ffn1_w_gate[i], ffn1_w_up[i], ffn1_w_down[i], ln_g[i, 0], ln_b[i, 0])
        j = i // 2
        wo = w_out[i].astype(BF16)
        if i % 2 == 0:
            lora = jnp.zeros((LORA_WIDTH, 3 * c), F32)
            lora = lora.at[:DECAY_LORA, :c].set(rwkv_w_up[j])
            lora = lora.at[DECAY_LORA:DECAY_LORA + AAA_LORA, c:2 * c].set(rwkv_a_up[j])
            lora = lora.at[DECAY_LORA + AAA_LORA:, 2 * c:].set(rwkv_g_up[j])
            r, lw, k, v, kk, b, gate, bonus, mq = _rwkv_prep(
                xf, rwkv_w_in[j].astype(BF16), rwkv_mu[j][None], lora.astype(BF16),
                rwkv_w0[j][None], rwkv_a0[j][None], rwkv_k_k[j][None], rwkv_k_a[j][None],
                rwkv_r_k[j].reshape(1, c), bd, seq=seq, tm=256)
            sh = (bsz, seq, c)
            raw = _rwkv_chunk(r.reshape(sh), lw.reshape(sh), k.reshape(sh), v.reshape(sh),
                              kk.reshape(sh), b.reshape(sh)).reshape(bsz * seq, c)
            xf = _mix_out_ln((raw, bonus, gate, rwkv_lnx_g[j][None], rwkv_lnx_b[j][None], bd), mq, 0, mem_kv,
                             xf, wo, ln_g[i, 1][None], ln_b[i, 1][None], alpha=alpha, seq=seq, tm=tm, rwkv=True)
        else:
            p = _proj(xf, sb_w_in[j].astype(BF16), tm=tm, tn=640, out_dtype=BF16)
            mix = _sb_attention(p, bsz=bsz, seq=seq, c=c)
            xf = _mix_out_ln((mix,), p, (3 * c) // MEM_WIDTH, mem_kv, xf, wo, ln_g[i, 1][None], ln_b[i, 1][None],
                             alpha=alpha, seq=seq, tm=tm, rwkv=False)
        xf = ffn(xf, ffn2_w_gate[i], ffn2_w_up[i], ffn2_w_down[i], ln_g[i, 2], ln_b[i, 2])
    return xf.reshape(bsz, seq, d)
```

```python
import functools
import math

import jax
import jax.numpy as jnp
from jax import lax
from jax.experimental import pallas as pl
from jax.experimental.pallas import tpu as pltpu

F32 = jnp.float32
BF16 = jnp.bfloat16

HEAD_DIM = 64
N_MEM_HEADS = 4
MEM_WIDTH = N_MEM_HEADS * HEAD_DIM
DECAY_LORA = 64
AAA_LORA = 64
GATE_LORA = 128
LORA_WIDTH = DECAY_LORA + AAA_LORA + GATE_LORA
LN_EPS = 1e-5
LNX_EPS = 64e-5
RWKV_CHUNK = 64
RWKV_CHUNKS_PER_STEP = 2
INV_BLOCK = 16
SB_BLOCK = 256
SB_SUB = 128
SB_HEADS = 4
VMEM_LIMIT = 48 * 1024 * 1024


def _cparams(*sem):
    return pltpu.CompilerParams(dimension_semantics=sem, vmem_limit_bytes=VMEM_LIMIT)


def _dot(a, b):
    return jnp.dot(a, b, preferred_element_type=F32)


def _dot_nt(a, b):
    return lax.dot_general(a, b, (((1,), (1,)), ((), ())), preferred_element_type=F32)


def _dot_tn(a, b):
    return lax.dot_general(a, b, (((0,), (0,)), ((), ())), preferred_element_type=F32)


def _split_dot(x, ones, terms):
    acc = None
    rem = x
    for _ in range(terms):
        piece = rem.astype(BF16)
        part = _dot(piece, ones)
        acc = part if acc is None else acc + part
        rem = rem - piece.astype(F32)
    return acc


def _softplus(x):
    return jnp.maximum(x, 0.0) + jnp.log(1.0 + jnp.exp(-jnp.abs(x)))


def _layer_norm(y, g, b):
    mu = jnp.mean(y, axis=-1, keepdims=True)
    d = y - mu
    var = jnp.mean(d * d, axis=-1, keepdims=True)
    return d * lax.rsqrt(var + LN_EPS) * g + b


def _ffn_kernel(x_ref, wg_ref, wu_ref, wd_ref, g_ref, b_ref, o_ref, xb_ref, acc_ref, *, alpha):
    j = pl.program_id(1)

    @pl.when(j == 0)
    def _():
        xb_ref[...] = x_ref[...].astype(BF16)
        acc_ref[...] = jnp.zeros_like(acc_ref)

    xb = xb_ref[...]
    gate = _dot(xb, wg_ref[...])
    up = _dot(xb, wu_ref[...])
    h = gate * jax.nn.sigmoid(gate) * up
    acc_ref[...] += _dot(h.astype(BF16), wd_ref[...])

    @pl.when(j == pl.num_programs(1) - 1)
    def _():
        y = alpha * x_ref[...] + 0.5 * acc_ref[...]
        o_ref[...] = _layer_norm(y, g_ref[...], b_ref[...])


def _ffn_ln(x, wg, wu, wd, g, b, *, alpha, tm, tf):
    t, d = x.shape
    f = wg.shape[1]
    return pl.pallas_call(
        functools.partial(_ffn_kernel, alpha=alpha),
        grid=(t // tm, f // tf),
        in_specs=[
            pl.BlockSpec((tm, d), lambda i, j: (i, 0)),
            pl.BlockSpec((d, tf), lambda i, j: (0, j)),
            pl.BlockSpec((d, tf), lambda i, j: (0, j)),
            pl.BlockSpec((tf, d), lambda i, j: (j, 0)),
            pl.BlockSpec((1, d), lambda i, j: (0, 0)),
            pl.BlockSpec((1, d), lambda i, j: (0, 0)),
        ],
        out_specs=pl.BlockSpec((tm, d), lambda i, j: (i, 0)),
        out_shape=jax.ShapeDtypeStruct((t, d), F32),
        scratch_shapes=[pltpu.VMEM((tm, d), BF16), pltpu.VMEM((tm, d), F32)],
        compiler_params=_cparams("parallel", "arbitrary"),
        name="ffn_ln",
    )(x, wg, wu, wd, g, b)


def _proj_kernel(x_ref, w_ref, s_ref, o_ref):
    o_ref[...] = (_dot(x_ref[...].astype(BF16), w_ref[...]) * s_ref[...]).astype(o_ref.dtype)


def _proj(x, w, col_scale, *, tm, tn, out_dtype):
    t, d = x.shape
    n = w.shape[1]
    return pl.pallas_call(
        _proj_kernel,
        grid=(t // tm, n // tn),
        in_specs=[pl.BlockSpec((tm, d), lambda i, j: (i, 0)),
                  pl.BlockSpec((d, tn), lambda i, j: (0, j)),
                  pl.BlockSpec((1, tn), lambda i, j: (0, j))],
        out_specs=pl.BlockSpec((tm, tn), lambda i, j: (i, j)),
        out_shape=jax.ShapeDtypeStruct((t, n), out_dtype),
        compiler_params=_cparams("parallel", "parallel"),
        name="proj",
    )(x, w, col_scale)


def _rwkv_prep_kernel(x_ref, win_ref, mu_ref, lora_ref, w0_ref, a0_ref, kkw_ref, kaw_ref, rkw_ref, bd_ref,
                      r_o, lw_o, k_o, v_o, kk_o, b_o, g_o, bonus_o, mq_o, carry_ref, *, tiles_per_seq, c):
    i = pl.program_id(0)
    shift = 3 * c + LORA_WIDTH

    @pl.when(i % tiles_per_seq == 0)
    def _():
        carry_ref[...] = jnp.zeros_like(carry_ref)

    p = _dot(x_ref[...].astype(BF16), win_ref[...])
    tm = p.shape[0]
    ps = p[:, :shift]
    mq_o[...] = p[:, shift:].astype(mq_o.dtype)

    row = lax.broadcasted_iota(jnp.int32, ps.shape, 0)
    prev = jnp.where(row == 0, carry_ref[...], pltpu.roll(ps, 1, 0))
    carry_ref[...] = ps[tm - 1:tm, :]
    z = ps + (prev - ps) * mu_ref[...]

    r = z[:, :c]
    kraw = z[:, c:2 * c]
    v = z[:, 2 * c:3 * c]
    lo = z[:, 3 * c:]
    lane = lax.broadcasted_iota(jnp.int32, lo.shape, 1)
    lhs = jnp.where(lane < DECAY_LORA, jnp.tanh(lo),
                    jnp.where(lane < DECAY_LORA + AAA_LORA, lo, jax.nn.sigmoid(lo)))
    pre = _dot(lhs.astype(BF16), lora_ref[...])
    w = -_softplus(-(pre[:, :c] + w0_ref[...])) - 0.5
    a = jax.nn.sigmoid(pre[:, c:2 * c] + a0_ref[...])
    g_o[...] = pre[:, 2 * c:]

    bd = bd_ref[...]
    kkv = kraw * kkw_ref[...]
    ss = _split_dot(kkv * kkv, bd, 2)
    kkn = kkv * lax.rsqrt(jnp.maximum(ss, 1e-24))
    k2 = kraw * (1.0 + (a - 1.0) * kaw_ref[...])

    r_o[...] = r
    lw_o[...] = -jnp.exp(w)
    k_o[...] = k2
    v_o[...] = v
    kk_o[...] = kkn
    b_o[...] = kkn * a
    bonus_o[...] = _split_dot(r * k2 * rkw_ref[...], bd, 2) * v


def _rwkv_prep(x, w_in, mu, lora, w0, a0, k_k, k_a, r_k, bd, *, seq, tm):
    t, d = x.shape
    c = w0.shape[1]
    n_in = w_in.shape[1]
    shift = 3 * c + LORA_WIDTH
    row = lambda i: (i, 0)
    const = lambda i: (0, 0)
    wide = jax.ShapeDtypeStruct((t, c), F32)
    return pl.pallas_call(
        functools.partial(_rwkv_prep_kernel, tiles_per_seq=seq // tm, c=c),
        grid=(t // tm,),
        in_specs=[
            pl.BlockSpec((tm, d), row),
            pl.BlockSpec((d, n_in), const),
            pl.BlockSpec((1, shift), const),
            pl.BlockSpec((LORA_WIDTH, 3 * c), const),
            pl.BlockSpec((1, c), const), pl.BlockSpec((1, c), const), pl.BlockSpec((1, c), const),
            pl.BlockSpec((1, c), const), pl.BlockSpec((1, c), const),
            pl.BlockSpec((c, c), const),
        ],
        out_specs=[pl.BlockSpec((tm, c), row)] * 8 + [pl.BlockSpec((tm, MEM_WIDTH), row)],
        out_shape=[wide] * 8 + [jax.ShapeDtypeStruct((t, MEM_WIDTH), BF16)],
        scratch_shapes=[pltpu.VMEM((1, shift), F32)],
        compiler_params=_cparams("arbitrary"),
        name="rwkv_prep",
    )(x, w_in, mu, lora, w0, a0, k_k, k_a, r_k, bd)


def _unit_lower_inverse(ms, eye, same_block):
    md = [jnp.where(same_block, m, 0.0) for m in ms]
    n = [m - d for m, d in zip(ms, md)]
    m2 = [_dot(d, d) for d in md]
    m4 = [_dot(x, x) for x in m2]
    m8 = [_dot(x, x) for x in m4]
    di = [eye - d for d in md]
    di = [x + _dot(x, p) for x, p in zip(di, m2)]
    di = [x + _dot(x, p) for x, p in zip(di, m4)]
    di = [x + _dot(x, p) for x, p in zip(di, m8)]
    xn = [_dot(x, y) for x, y in zip(di, n)]
    x2 = [_dot(x, x) for x in xn]
    y = [eye - x for x in xn]
    y = [a + _dot(a, p) for a, p in zip(y, x2)]
    return [_dot(a, x) for a, x in zip(y, di)]


def _rwkv_chunk_kernel(r_ref, lw_ref, k_ref, v_ref, kk_ref, b_ref, o_ref, s_ref, *, n_heads, n_chunks):
    @pl.when(pl.program_id(1) == 0)
    def _():
        s_ref[...] = jnp.zeros_like(s_ref)

    L = RWKV_CHUNK
    ti = lax.broadcasted_iota(jnp.int32, (L, L), 0)
    si = lax.broadcasted_iota(jnp.int32, (L, L), 1)
    incl = si <= ti
    strict = si < ti
    eye = (si == ti).astype(F32)
    same_block = (si // INV_BLOCK) == (ti // INV_BLOCK)
    tril_ones = incl.astype(BF16)
    heads = [slice(h * HEAD_DIM, (h + 1) * HEAD_DIM) for h in range(n_heads)]

    pairs = []
    for ci in range(n_chunks):
        rows = slice(ci * L, (ci + 1) * L)
        lw = lw_ref[0, rows, :]
        cum = _split_dot_left(tril_ones, lw, 3)
        cum_last = cum[L - 1:L, :]
        g_inv = jnp.exp(-cum)
        g_rel = jnp.exp(cum_last - cum)
        kk = kk_ref[0, rows, :]
        b = b_ref[0, rows, :]
        k = k_ref[0, rows, :]
        kap = kk * jnp.exp(cum - lw)
        rho = r_ref[0, rows, :] * jnp.exp(cum)
        bet = b * g_inv
        kt = k * g_inv
        bg = b * g_rel
        kg = k * g_rel
        v = v_ref[0, rows, :]
        g_last = jnp.exp(cum_last)
        for sl in heads:
            pairs.append(dict(kap=kap[:, sl], rho=rho[:, sl], bet=bet[:, sl], kt=kt[:, sl], bg=bg[:, sl],
                              kg=kg[:, sl], v=v[:, sl], g_last=g_last[:, sl]))
    m_ab = [jnp.where(strict, _dot_nt(p["kap"], p["bet"]), 0.0) for p in pairs]
    m_ak = [jnp.where(strict, _dot_nt(p["kap"], p["kt"]), 0.0) for p in pairs]
    a_rb = [jnp.where(incl, _dot_nt(p["rho"], p["bet"]), 0.0) for p in pairs]
    a_rk = [jnp.where(incl, _dot_nt(p["rho"], p["kt"]), 0.0) for p in pairs]
    t_inv = _unit_lower_inverse(m_ab, eye, same_block)
    mv = [_dot(m, p["v"]) for m, p in zip(m_ak, pairs)]
    w = [_dot(t, p["kap"]) for t, p in zip(t_inv, pairs)]
    u0 = [-_dot(t, x) for t, x in zip(t_inv, mv)]
    o0 = [_dot(a, p["v"]) for a, p in zip(a_rk, pairs)]
    s_kv = [_dot_tn(p["v"], p["kg"]) for p in pairs]

    state = [s_ref[h] for h in range(n_heads)]
    for ci in range(n_chunks):
        idx = range(ci * n_heads, (ci + 1) * n_heads)
        u = [u0[i] - _dot_nt(w[i], s) for i, s in zip(idx, state)]
        rs = [_dot_nt(pairs[i]["rho"], s) for i, s in zip(idx, state)]
        o = [o0[i] + x + _dot(a_rb[i], y) for i, x, y in zip(idx, rs, u)]
        state = [s * pairs[i]["g_last"] + _dot_tn(y, pairs[i]["bg"]) + s_kv[i]
                 for i, s, y in zip(idx, state, u)]
        o_ref[0, ci * L:(ci + 1) * L, :] = jnp.concatenate(o, axis=1)
    for h in range(n_heads):
        s_ref[h] = state[h]


def _split_dot_left(ones, x, terms):
    acc = None
    rem = x
    for _ in range(terms):
        piece = rem.astype(BF16)
        part = _dot(ones, piece)
        acc = part if acc is None else acc + part
        rem = rem - piece.astype(F32)
    return acc


def _rwkv_chunk(r, lw, k, v, kk, b, *, n_chunks):
    bsz, seq, c = r.shape
    n_heads = c // HEAD_DIM
    rows = n_chunks * RWKV_CHUNK
    blk = pl.BlockSpec((1, rows, c), lambda bi, ci: (bi, ci, 0))
    return pl.pallas_call(
        functools.partial(_rwkv_chunk_kernel, n_heads=n_heads, n_chunks=n_chunks),
        grid=(bsz, seq // rows),
        in_specs=[blk] * 6,
        out_specs=blk,
        out_shape=jax.ShapeDtypeStruct((bsz, seq, c), F32),
        scratch_shapes=[pltpu.VMEM((n_heads, HEAD_DIM, HEAD_DIM), F32)],
        compiler_params=_cparams("parallel", "arbitrary"),
        name="rwkv_chunk",
    )(r, lw, k, v, kk, b)


def _sb_kernel(q_ref, k_ref, v_ref, o_ref, *, n_heads):
    qi = pl.program_id(2)
    tq = q_ref.shape[0]
    sub = SB_SUB
    n_sub = tq // sub
    jr = lax.broadcasted_iota(jnp.int32, (2 * sub, 2 * sub), 0) % sub
    sc = lax.broadcasted_iota(jnp.int32, (2 * sub, 2 * sub), 1)
    sums = ((sc >= sub) | (jr > sc)).astype(BF16)
    ti = lax.broadcasted_iota(jnp.int32, (tq, tq), 0)
    si = lax.broadcasted_iota(jnp.int32, (tq, tq), 1)
    before = si < ti
    q_all = q_ref[...]

    def key_block(j, carries, accs, diagonal):
        rows = pl.ds(pl.multiple_of(j * tq, tq), tq)
        kb = k_ref[rows, :]
        vb = v_ref[rows, :]
        new_carries, new_accs = [], []
        for h in range(n_heads):
            sl = slice(h * HEAD_DIM, (h + 1) * HEAD_DIM)
            y = _dot_nt(q_all[:, sl], kb[:, sl])
            sp = jnp.maximum(y, 0.0) + jnp.log2(1.0 + jnp.exp2(-jnp.abs(y)))
            if diagonal:
                sp = jnp.where(before, sp, 0.0)
            carry = carries[h]
            att = [None] * n_sub
            for u in reversed(range(n_sub)):
                ul = slice(u * sub, (u + 1) * sub)
                spu = sp[:, ul]
                hi = spu.astype(BF16)
                lo = (spu - hi.astype(F32)).astype(BF16)
                out = _dot(jnp.concatenate([hi, lo], axis=1), sums)
                later = out[:, :sub] + carry
                carry = carry + out[:, sub:]
                att[u] = jnp.exp2(y[:, ul] - spu - later)
            att = jnp.concatenate(att, axis=1)
            if diagonal:
                att = jnp.where(before, att, 0.0)
            new_accs.append(accs[h] + _dot(att.astype(BF16), vb[:, sl]))
            new_carries.append(carry)
        return tuple(new_carries), tuple(new_accs)

    zeros_c = tuple(jnp.zeros((tq, sub), F32) for _ in range(n_heads))
    zeros_a = tuple(jnp.zeros((tq, HEAD_DIM), F32) for _ in range(n_heads))
    carries, accs = key_block(qi, zeros_c, zeros_a, True)

    def body(jj, state):
        return key_block(qi - 1 - jj, state[0], state[1], False)

    _, accs = lax.fori_loop(0, qi, body, (carries, accs))
    o_ref[...] = jnp.concatenate(accs, axis=1).astype(o_ref.dtype)


def _sb_attention(p, *, bsz, seq, c):
    pair = SB_HEADS * HEAD_DIM
    n_pairs = c // pair
    nq = seq // SB_BLOCK
    return pl.pallas_call(
        functools.partial(_sb_kernel, n_heads=SB_HEADS),
        grid=(bsz, n_pairs, nq),
        in_specs=[
            pl.BlockSpec((SB_BLOCK, pair), lambda b, h, i: (b * nq + i, h)),
            pl.BlockSpec((seq, pair), lambda b, h, i: (b, n_pairs + h)),
            pl.BlockSpec((seq, pair), lambda b, h, i: (b, 2 * n_pairs + h)),
        ],
        out_specs=pl.BlockSpec((SB_BLOCK, pair), lambda b, h, i: (b * nq + i, h)),
        out_shape=jax.ShapeDtypeStruct((bsz * seq, c), BF16),
        compiler_params=_cparams("parallel", "parallel", "arbitrary"),
        name="sb_attention",
    )(p, p, p)


def _memory_attention(mq, mem_kv):
    scale = 1.0 / math.sqrt(HEAD_DIM)
    outs = []
    for h in range(N_MEM_HEADS):
        sl = slice(h * HEAD_DIM, (h + 1) * HEAD_DIM)
        vl = slice(MEM_WIDTH + h * HEAD_DIM, MEM_WIDTH + (h + 1) * HEAD_DIM)
        s = _dot_nt(mq[:, sl], mem_kv[:, sl]) * scale
        e = jnp.exp(s - jnp.max(s, axis=-1, keepdims=True))
        prob = e / jnp.sum(e, axis=-1, keepdims=True)
        outs.append(_dot(prob.astype(BF16), mem_kv[:, vl]))
    return jnp.concatenate(outs, axis=1)


def _out_ln(mix_bf16, mq_ref, memkv_ref, x_ref, wout_ref, g_ref, b_ref, o_ref, *, alpha):
    c = mix_bf16.shape[1]
    mem_o = _memory_attention(mq_ref[...], memkv_ref[0])
    y = _dot(mix_bf16, wout_ref[:c, :]) + _dot(mem_o.astype(BF16), wout_ref[c:, :])
    o_ref[...] = _layer_norm(alpha * x_ref[...] + y, g_ref[...], b_ref[...])


def _sb_out_kernel(mix_ref, mq_ref, memkv_ref, x_ref, wout_ref, g_ref, b_ref, o_ref, *, alpha):
    _out_ln(mix_ref[...], mq_ref, memkv_ref, x_ref, wout_ref, g_ref, b_ref, o_ref, alpha=alpha)


def _rwkv_out_kernel(raw_ref, bonus_ref, gate_ref, lg_ref, lb_ref, bd_ref,
                     mq_ref, memkv_ref, x_ref, wout_ref, g_ref, b_ref, o_ref, *, alpha):
    raw = raw_ref[...]
    bd = bd_ref[...]
    inv_n = 1.0 / HEAD_DIM
    m = _split_dot(raw, bd, 3) * inv_n
    d = raw - m
    var = _split_dot(d * d, bd, 2) * inv_n
    normed = d * lax.rsqrt(var + LNX_EPS) * lg_ref[...] + lb_ref[...]
    mix = (normed + bonus_ref[...]) * gate_ref[...]
    _out_ln(mix.astype(BF16), mq_ref, memkv_ref, x_ref, wout_ref, g_ref, b_ref, o_ref, alpha=alpha)


def _mix_out_ln(mix_inputs, mq, mq_col, mem_kv, x, w_out, g, b, *, alpha, seq, tm, rwkv):
    t, d = x.shape
    tiles_per_seq = seq // tm
    row = lambda i: (i, 0)
    const = lambda i: (0, 0)
    c = d - MEM_WIDTH
    if rwkv:
        kern = _rwkv_out_kernel
        mix_specs = [pl.BlockSpec((tm, c), row)] * 3 + [pl.BlockSpec((1, c), const)] * 2 + [pl.BlockSpec((c, c), const)]
    else:
        kern = _sb_out_kernel
        mix_specs = [pl.BlockSpec((tm, c), row)]
    return pl.pallas_call(
        functools.partial(kern, alpha=alpha),
        grid=(t // tm,),
        in_specs=mix_specs + [
            pl.BlockSpec((tm, MEM_WIDTH), lambda i: (i, mq_col)),
            pl.BlockSpec((1,) + mem_kv.shape[1:], lambda i: (i // tiles_per_seq, 0, 0)),
            pl.BlockSpec((tm, d), row),
            pl.BlockSpec((d, d), const),
            pl.BlockSpec((1, d), const),
            pl.BlockSpec((1, d), const),
        ],
        out_specs=pl.BlockSpec((tm, d), row),
        out_shape=jax.ShapeDtypeStruct((t, d), F32),
        compiler_params=_cparams("parallel"),
        name="rwkv_out_ln" if rwkv else "sb_out_ln",
    )(*mix_inputs, mq, mem_kv, x, w_out, g, b)


def kernel(x, mem, ffn1_w_gate, ffn1_w_up, ffn1_w_down, ffn2_w_gate, ffn2_w_up, ffn2_w_down, ln_g, ln_b, w_out, w_mem_kv, rwkv_w_in, rwkv_mu, rwkv_w0, rwkv_w_up, rwkv_a0, rwkv_a_up, rwkv_g_up, rwkv_k_k, rwkv_k_a, rwkv_r_k, rwkv_lnx_g, rwkv_lnx_b, sb_w_in):
    bsz, seq, d = x.shape
    depth = ln_g.shape[0]
    c = d - MEM_WIDTH
    alpha = (2 * depth) ** 0.25
    n_mem = mem.shape[1]
    tm = min(512, seq)
    tf = 256

    head_of = jnp.arange(c) // HEAD_DIM
    bd = (head_of[:, None] == head_of[None, :]).astype(BF16)

    mem_kv = _proj(mem.reshape(bsz * n_mem, d), w_mem_kv.astype(BF16), jnp.ones((1, 2 * MEM_WIDTH), F32),
                   tm=256, tn=2 * MEM_WIDTH, out_dtype=BF16)
    mem_kv = mem_kv.reshape(bsz, n_mem, 2 * MEM_WIDTH)

    def ffn(xf, wg, wu, wd, g, b):
        return _ffn_ln(xf, wg.astype(BF16), wu.astype(BF16), wd.astype(BF16), g[None], b[None],
                       alpha=alpha, tm=tm, tf=tf)

    xf = x.reshape(bsz * seq, d)
    for i in range(depth):
        xf = ffn(xf, ffn1_w_gate[i], ffn1_w_up[i], ffn1_w_down[i], ln_g[i, 0], ln_b[i, 0])
        j = i // 2
        wo = w_out[i].astype(BF16)
        if i % 2 == 0:
            lora = jnp.zeros((LORA_WIDTH, 3 * c), F32)
            lora = lora.at[:DECAY_LORA, :c].set(rwkv_w_up[j])
            lora = lora.at[DECAY_LORA:DECAY_LORA + AAA_LORA, c:2 * c].set(rwkv_a_up[j])
            lora = lora.at[DECAY_LORA + AAA_LORA:, 2 * c:].set(rwkv_g_up[j])
            r, lw, k, v, kk, b, gate, bonus, mq = _rwkv_prep(
                xf, rwkv_w_in[j].astype(BF16), rwkv_mu[j][None], lora.astype(BF16),
                rwkv_w0[j][None], rwkv_a0[j][None], rwkv_k_k[j][None], rwkv_k_a[j][None],
                rwkv_r_k[j].reshape(1, c), bd, seq=seq, tm=256)
            sh = (bsz, seq, c)
            raw = _rwkv_chunk(r.reshape(sh), lw.reshape(sh), k.reshape(sh), v.reshape(sh),
                              kk.reshape(sh), b.reshape(sh), n_chunks=RWKV_CHUNKS_PER_STEP).reshape(bsz * seq, c)
            xf = _mix_out_ln((raw, bonus, gate, rwkv_lnx_g[j][None], rwkv_lnx_b[j][None], bd), mq, 0, mem_kv,
                             xf, wo, ln_g[i, 1][None], ln_b[i, 1][None], alpha=alpha, seq=seq, tm=tm, rwkv=True)
        else:
            q_scale = math.log2(math.e) / math.sqrt(HEAD_DIM)
            col_scale = jnp.where(jnp.arange(3 * c + MEM_WIDTH) < c, q_scale, 1.0).astype(F32)[None]
            p = _proj(xf, sb_w_in[j].astype(BF16), col_scale, tm=tm, tn=640, out_dtype=BF16)
            mix = _sb_attention(p, bsz=bsz, seq=seq, c=c)
            xf = _mix_out_ln((mix,), p, (3 * c) // MEM_WIDTH, mem_kv, xf, wo, ln_g[i, 1][None], ln_b[i, 1][None],
                             alpha=alpha, seq=seq, tm=tm, rwkv=False)
        xf = ffn(xf, ffn2_w_gate[i], ffn2_w_up[i], ffn2_w_down[i], ln_g[i, 2], ln_b[i, 2])
    return xf.reshape(bsz, seq, d)
```

```python
import functools
import math

import jax
import jax.numpy as jnp
from jax import lax
from jax.experimental import pallas as pl
from jax.experimental.pallas import tpu as pltpu

F32 = jnp.float32
BF16 = jnp.bfloat16

HEAD_DIM = 64
N_MEM_HEADS = 4
MEM_WIDTH = N_MEM_HEADS * HEAD_DIM
DECAY_LORA = 64
AAA_LORA = 64
GATE_LORA = 128
LORA_WIDTH = DECAY_LORA + AAA_LORA + GATE_LORA
LN_EPS = 1e-5
LNX_EPS = 64e-5
RWKV_CHUNK = 64
RWKV_CHUNKS_PER_STEP = 4
INV_BLOCK = 16
SB_BLOCK = 256
SB_SUB = 128
SB_HEADS = 12
VMEM_LIMIT = 48 * 1024 * 1024


def _cparams(*sem):
    return pltpu.CompilerParams(dimension_semantics=sem, vmem_limit_bytes=VMEM_LIMIT)


def _dot(a, b):
    return jnp.dot(a, b, preferred_element_type=F32)


def _dot_nt(a, b):
    return lax.dot_general(a, b, (((1,), (1,)), ((), ())), preferred_element_type=F32)


def _dot_tn(a, b):
    return lax.dot_general(a, b, (((0,), (0,)), ((), ())), preferred_element_type=F32)


def _split_dot(x, ones, terms):
    acc = None
    rem = x
    for _ in range(terms):
        piece = rem.astype(BF16)
        part = _dot(piece, ones)
        acc = part if acc is None else acc + part
        rem = rem - piece.astype(F32)
    return acc


def _softplus(x):
    return jnp.maximum(x, 0.0) + jnp.log(1.0 + jnp.exp(-jnp.abs(x)))


def _layer_norm(y, g, b):
    mu = jnp.mean(y, axis=-1, keepdims=True)
    d = y - mu
    var = jnp.mean(d * d, axis=-1, keepdims=True)
    return d * lax.rsqrt(var + LN_EPS) * g + b


def _ffn_kernel(x_ref, wg_ref, wu_ref, wd_ref, g_ref, b_ref, o_ref, h_ref, *, alpha, tf):
    x = x_ref[...]
    xb = x.astype(BF16)
    for j in range(wg_ref.shape[1] // tf):
        cols = slice(j * tf, (j + 1) * tf)
        gate = _dot(xb, wg_ref[:, cols])
        up = _dot(xb, wu_ref[:, cols])
        h_ref[:, cols] = (gate * jax.nn.sigmoid(gate) * up).astype(BF16)
    y = alpha * x + 0.5 * _dot(h_ref[...], wd_ref[...])
    o_ref[...] = _layer_norm(y, g_ref[...], b_ref[...])


def _resident(shape):
    return pl.BlockSpec(shape, lambda i: (0,) * len(shape), pipeline_mode=pl.Buffered(1))


def _ffn_ln(x, wg, wu, wd, g, b, *, alpha, tm, tf):
    t, d = x.shape
    f = wg.shape[1]
    return pl.pallas_call(
        functools.partial(_ffn_kernel, alpha=alpha, tf=tf),
        grid=(t // tm,),
        in_specs=[
            pl.BlockSpec((tm, d), lambda i: (i, 0)),
            _resident((d, f)), _resident((d, f)), _resident((f, d)),
            _resident((1, d)), _resident((1, d)),
        ],
        out_specs=pl.BlockSpec((tm, d), lambda i: (i, 0)),
        out_shape=jax.ShapeDtypeStruct((t, d), F32),
        scratch_shapes=[pltpu.VMEM((tm, f), BF16)],
        compiler_params=_cparams("parallel"),
        name="ffn_ln",
    )(x, wg, wu, wd, g, b)


def _proj_kernel(x_ref, w_ref, s_ref, o_ref):
    o_ref[...] = (_dot(x_ref[...].astype(BF16), w_ref[...]) * s_ref[...]).astype(o_ref.dtype)


def _proj(x, w, col_scale, *, tm, out_dtype):
    t, d = x.shape
    n = w.shape[1]
    return pl.pallas_call(
        _proj_kernel,
        grid=(t // tm,),
        in_specs=[pl.BlockSpec((tm, d), lambda i: (i, 0)), _resident((d, n)), _resident((1, n))],
        out_specs=pl.BlockSpec((tm, n), lambda i: (i, 0)),
        out_shape=jax.ShapeDtypeStruct((t, n), out_dtype),
        compiler_params=_cparams("parallel"),
        name="proj",
    )(x, w, col_scale)


def _rwkv_prep_kernel(x_ref, win_ref, mu_ref, lora_ref, w0_ref, a0_ref, kkw_ref, kaw_ref, rkw_ref, bd_ref,
                      r_o, lw_o, k_o, v_o, kk_o, b_o, g_o, bonus_o, mq_o, carry_ref, *, tiles_per_seq, c):
    i = pl.program_id(0)
    shift = 3 * c + LORA_WIDTH

    @pl.when(i % tiles_per_seq == 0)
    def _():
        carry_ref[...] = jnp.zeros_like(carry_ref)

    p = _dot(x_ref[...].astype(BF16), win_ref[...])
    tm = p.shape[0]
    ps = p[:, :shift]
    mq_o[...] = p[:, shift:].astype(mq_o.dtype)

    row = lax.broadcasted_iota(jnp.int32, ps.shape, 0)
    prev = jnp.where(row == 0, carry_ref[...], pltpu.roll(ps, 1, 0))
    carry_ref[...] = ps[tm - 1:tm, :]
    z = ps + (prev - ps) * mu_ref[...]

    r = z[:, :c]
    kraw = z[:, c:2 * c]
    v = z[:, 2 * c:3 * c]
    lo = z[:, 3 * c:]
    lane = lax.broadcasted_iota(jnp.int32, lo.shape, 1)
    lhs = jnp.where(lane < DECAY_LORA, jnp.tanh(lo),
                    jnp.where(lane < DECAY_LORA + AAA_LORA, lo, jax.nn.sigmoid(lo)))
    pre = _dot(lhs.astype(BF16), lora_ref[...])
    w = -_softplus(-(pre[:, :c] + w0_ref[...])) - 0.5
    a = jax.nn.sigmoid(pre[:, c:2 * c] + a0_ref[...])
    g_o[...] = pre[:, 2 * c:]

    bd = bd_ref[...]
    kkv = kraw * kkw_ref[...]
    ss = _split_dot(kkv * kkv, bd, 2)
    kkn = kkv * lax.rsqrt(jnp.maximum(ss, 1e-24))
    k2 = kraw * (1.0 + (a - 1.0) * kaw_ref[...])

    r_o[...] = r
    lw_o[...] = -jnp.exp(w)
    k_o[...] = k2
    v_o[...] = v
    kk_o[...] = kkn
    b_o[...] = kkn * a
    bonus_o[...] = _split_dot(r * k2 * rkw_ref[...], bd, 2) * v


def _rwkv_prep(x, w_in, mu, lora, w0, a0, k_k, k_a, r_k, bd, *, seq, tm):
    t, d = x.shape
    c = w0.shape[1]
    n_in = w_in.shape[1]
    shift = 3 * c + LORA_WIDTH
    row = lambda i: (i, 0)
    const = lambda i: (0, 0)
    wide = jax.ShapeDtypeStruct((t, c), F32)
    return pl.pallas_call(
        functools.partial(_rwkv_prep_kernel, tiles_per_seq=seq // tm, c=c),
        grid=(t // tm,),
        in_specs=[
            pl.BlockSpec((tm, d), row),
            pl.BlockSpec((d, n_in), const),
            pl.BlockSpec((1, shift), const),
            pl.BlockSpec((LORA_WIDTH, 3 * c), const),
            pl.BlockSpec((1, c), const), pl.BlockSpec((1, c), const), pl.BlockSpec((1, c), const),
            pl.BlockSpec((1, c), const), pl.BlockSpec((1, c), const),
            pl.BlockSpec((c, c), const),
        ],
        out_specs=[pl.BlockSpec((tm, c), row)] * 8 + [pl.BlockSpec((tm, MEM_WIDTH), row)],
        out_shape=[wide] * 8 + [jax.ShapeDtypeStruct((t, MEM_WIDTH), BF16)],
        scratch_shapes=[pltpu.VMEM((1, shift), F32)],
        compiler_params=_cparams("arbitrary"),
        name="rwkv_prep",
    )(x, w_in, mu, lora, w0, a0, k_k, k_a, r_k, bd)


def _rwkv_chunk_kernel(r_ref, lw_ref, k_ref, v_ref, kk_ref, b_ref, o_ref, s_ref, *, n_pairs, n_chunks):
    @pl.when(pl.program_id(1) == 0)
    def _():
        s_ref[...] = jnp.zeros_like(s_ref)

    L = RWKV_CHUNK
    width = 2 * HEAD_DIM
    ti = lax.broadcasted_iota(jnp.int32, (L, width), 0)
    lane = lax.broadcasted_iota(jnp.int32, (L, width), 1)
    si = lane % HEAD_DIM
    first = lane < HEAD_DIM
    incl = si <= ti
    strict = si < ti
    eye = (si == ti).astype(F32)
    same_block = (si // INV_BLOCK) == (ti // INV_BLOCK)
    tri = lax.broadcasted_iota(jnp.int32, (L, L), 0) >= lax.broadcasted_iota(jnp.int32, (L, L), 1)
    tril_ones = tri.astype(BF16)

    def bdiag(x):
        return jnp.concatenate([jnp.where(first, x, 0.0), jnp.where(first, 0.0, x)], axis=0)

    def pdot(a, b):
        return _dot(a, bdiag(b))

    items = []
    for ci in range(n_chunks):
        rows = slice(ci * L, (ci + 1) * L)
        lw = lw_ref[0, rows, :]
        cum = _split_dot_left(tril_ones, lw, 3)
        cum_last = cum[L - 1:L, :]
        g_inv = jnp.exp(-cum)
        g_rel = jnp.exp(cum_last - cum)
        kk = kk_ref[0, rows, :]
        b = b_ref[0, rows, :]
        k = k_ref[0, rows, :]
        kap = kk * jnp.exp(cum - lw)
        rho = r_ref[0, rows, :] * jnp.exp(cum)
        bet = b * g_inv
        kt = k * g_inv
        bg = b * g_rel
        kg = k * g_rel
        v = v_ref[0, rows, :]
        g_last = jnp.exp(cum_last)
        for p in range(n_pairs):
            sl = slice(p * width, (p + 1) * width)
            items.append(dict(kap=kap[:, sl], rho=rho[:, sl], bet=bet[:, sl], kt=kt[:, sl], bg=bg[:, sl],
                              kg=kg[:, sl], v=v[:, sl], g_last=g_last[:, sl]))
    prod = [_dot_nt(jnp.concatenate([it["kap"], it["rho"]], axis=0),
                    jnp.concatenate([bdiag(it["bet"]), bdiag(it["kt"])], axis=0)) for it in items]
    m_ab = [jnp.where(strict, x[:L, :width], 0.0) for x in prod]
    m_ak = [jnp.where(strict, x[:L, width:], 0.0) for x in prod]
    a_r = [jnp.concatenate([jnp.where(incl, x[L:, :width], 0.0), jnp.where(incl, x[L:, width:], 0.0)], axis=1)
           for x in prod]

    md = [jnp.where(same_block, m, 0.0) for m in m_ab]
    nd = [m - d for m, d in zip(m_ab, md)]
    m2 = [pdot(d, d) for d in md]
    m4 = [pdot(x, x) for x in m2]
    m8 = [pdot(x, x) for x in m4]
    di = [eye - d for d in md]
    di = [x + pdot(x, q) for x, q in zip(di, m2)]
    di = [x + pdot(x, q) for x, q in zip(di, m4)]
    di = [x + pdot(x, q) for x, q in zip(di, m8)]
    xn = [pdot(x, y) for x, y in zip(di, nd)]
    x2 = [pdot(x, x) for x in xn]
    yi = [eye - x for x in xn]
    yi = [a + pdot(a, q) for a, q in zip(yi, x2)]
    t_inv = [pdot(a, x) for a, x in zip(yi, di)]

    bd_v = [bdiag(it["v"]) for it in items]
    mv = [_dot(m, x) for m, x in zip(m_ak, bd_v)]
    wu = [_dot(t, jnp.concatenate([bdiag(it["kap"]), bdiag(x)], axis=1))
          for t, it, x in zip(t_inv, items, mv)]

    state = [s_ref[p] for p in range(n_pairs)]
    for ci in range(n_chunks):
        idx = range(ci * n_pairs, (ci + 1) * n_pairs)
        ws = [_dot_nt(jnp.concatenate([wu[i][:, :width], items[i]["rho"]], axis=0), bdiag(s))
              for i, s in zip(idx, state)]
        u = [-wu[i][:, width:] - x[:L] for i, x in zip(idx, ws)]
        for n, (i, x, y) in enumerate(zip(idx, ws, u)):
            o_ref[0, ci * L:(ci + 1) * L, n * width:(n + 1) * width] = (
                x[L:] + _dot(a_r[i], jnp.concatenate([bdiag(y), bd_v[i]], axis=0)))
        upd = [_dot_tn(jnp.concatenate([y, items[i]["v"]], axis=0),
                       jnp.concatenate([items[i]["bg"], items[i]["kg"]], axis=0)) for i, y in zip(idx, u)]
        state = [s * items[i]["g_last"] + jnp.where(first, x[:L], x[L:]) for i, s, x in zip(idx, state, upd)]
    for p in range(n_pairs):
        s_ref[p] = state[p]


def _split_dot_left(ones, x, terms):
    acc = None
    rem = x
    for _ in range(terms):
        piece = rem.astype(BF16)
        part = _dot(ones, piece)
        acc = part if acc is None else acc + part
        rem = rem - piece.astype(F32)
    return acc


def _rwkv_chunk(r, lw, k, v, kk, b, *, n_chunks):
    bsz, seq, c = r.shape
    n_pairs = c // (2 * HEAD_DIM)
    rows = n_chunks * RWKV_CHUNK
    blk = pl.BlockSpec((1, rows, c), lambda bi, ci: (bi, ci, 0))
    return pl.pallas_call(
        functools.partial(_rwkv_chunk_kernel, n_pairs=n_pairs, n_chunks=n_chunks),
        grid=(bsz, seq // rows),
        in_specs=[blk] * 6,
        out_specs=blk,
        out_shape=jax.ShapeDtypeStruct((bsz, seq, c), F32),
        scratch_shapes=[pltpu.VMEM((n_pairs, HEAD_DIM, 2 * HEAD_DIM), F32)],
        compiler_params=_cparams("parallel", "arbitrary"),
        name="rwkv_chunk",
    )(r, lw, k, v, kk, b)


def _sb_kernel(q_ref, k_ref, v_ref, o_ref, *, n_heads):
    qi = pl.program_id(2)
    tq = q_ref.shape[0]
    sub = SB_SUB
    n_sub = tq // sub
    jr = lax.broadcasted_iota(jnp.int32, (2 * sub, 2 * sub), 0) % sub
    sc = lax.broadcasted_iota(jnp.int32, (2 * sub, 2 * sub), 1)
    sums = ((sc >= sub) | (jr >= sc)).astype(BF16)
    ti = lax.broadcasted_iota(jnp.int32, (tq, tq), 0)
    si = lax.broadcasted_iota(jnp.int32, (tq, tq), 1)
    before = si < ti
    q_all = q_ref[...]

    def key_block(j, carries, accs, diagonal):
        rows = pl.ds(pl.multiple_of(j * tq, tq), tq)
        kb = k_ref[rows, :]
        vb = v_ref[rows, :]
        ys, pieces = [], []
        for h in range(n_heads):
            sl = slice(h * HEAD_DIM, (h + 1) * HEAD_DIM)
            y = _dot_nt(q_all[:, sl], kb[:, sl])
            sp = jnp.maximum(y, 0.0) + jnp.log2(1.0 + jnp.exp2(-jnp.abs(y)))
            if diagonal:
                sp = jnp.where(before, sp, 0.0)
            ys.append(y)
            for u in range(n_sub):
                spu = sp[:, u * sub:(u + 1) * sub]
                hi = spu.astype(BF16)
                lo = (spu - hi.astype(F32)).astype(BF16)
                pieces.append(jnp.concatenate([hi, lo], axis=1))
        out = _dot(jnp.concatenate(pieces, axis=0), sums)
        new_carries, new_accs = [], []
        for h in range(n_heads):
            sl = slice(h * HEAD_DIM, (h + 1) * HEAD_DIM)
            carry = carries[h]
            att = [None] * n_sub
            for u in reversed(range(n_sub)):
                o_hu = out[(h * n_sub + u) * tq:(h * n_sub + u + 1) * tq, :]
                att[u] = jnp.exp2(ys[h][:, u * sub:(u + 1) * sub] - o_hu[:, :sub] - carry)
                carry = carry + o_hu[:, sub:]
            att = jnp.concatenate(att, axis=1)
            if diagonal:
                att = jnp.where(before, att, 0.0)
            new_accs.append(accs[h] + _dot(att.astype(BF16), vb[:, sl]))
            new_carries.append(carry)
        return tuple(new_carries), tuple(new_accs)

    zeros_c = tuple(jnp.zeros((tq, sub), F32) for _ in range(n_heads))
    zeros_a = tuple(jnp.zeros((tq, HEAD_DIM), F32) for _ in range(n_heads))
    carries, accs = key_block(qi, zeros_c, zeros_a, True)

    def body(jj, state):
        return key_block(qi - 1 - jj, state[0], state[1], False)

    _, accs = lax.fori_loop(0, qi, body, (carries, accs))
    o_ref[...] = jnp.concatenate(accs, axis=1).astype(o_ref.dtype)


def _sb_attention(p, *, bsz, seq, c):
    pair = SB_HEADS * HEAD_DIM
    n_pairs = c // pair
    nq = seq // SB_BLOCK
    return pl.pallas_call(
        functools.partial(_sb_kernel, n_heads=SB_HEADS),
        grid=(bsz, n_pairs, nq),
        in_specs=[
            pl.BlockSpec((SB_BLOCK, pair), lambda b, h, i: (b * nq + i, h)),
            pl.BlockSpec((seq, pair), lambda b, h, i: (b, n_pairs + h)),
            pl.BlockSpec((seq, pair), lambda b, h, i: (b, 2 * n_pairs + h)),
        ],
        out_specs=pl.BlockSpec((SB_BLOCK, pair), lambda b, h, i: (b * nq + i, h)),
        out_shape=jax.ShapeDtypeStruct((bsz * seq, c), BF16),
        compiler_params=_cparams("parallel", "parallel", "arbitrary"),
        name="sb_attention",
    )(p, p, p)


def _memory_attention(mq, mem_kv):
    scale = 1.0 / math.sqrt(HEAD_DIM)
    outs = []
    for h in range(N_MEM_HEADS):
        sl = slice(h * HEAD_DIM, (h + 1) * HEAD_DIM)
        vl = slice(MEM_WIDTH + h * HEAD_DIM, MEM_WIDTH + (h + 1) * HEAD_DIM)
        s = _dot_nt(mq[:, sl], mem_kv[:, sl]) * scale
        e = jnp.exp(s - jnp.max(s, axis=-1, keepdims=True))
        prob = e / jnp.sum(e, axis=-1, keepdims=True)
        outs.append(_dot(prob.astype(BF16), mem_kv[:, vl]))
    return jnp.concatenate(outs, axis=1)


def _out_ln(mix_bf16, mq_ref, memkv_ref, x_ref, wout_ref, g_ref, b_ref, o_ref, *, alpha):
    c = mix_bf16.shape[1]
    mem_o = _memory_attention(mq_ref[...], memkv_ref[0])
    y = _dot(mix_bf16, wout_ref[:c, :]) + _dot(mem_o.astype(BF16), wout_ref[c:, :])
    o_ref[...] = _layer_norm(alpha * x_ref[...] + y, g_ref[...], b_ref[...])


def _sb_out_kernel(mix_ref, mq_ref, memkv_ref, x_ref, wout_ref, g_ref, b_ref, o_ref, *, alpha):
    _out_ln(mix_ref[...], mq_ref, memkv_ref, x_ref, wout_ref, g_ref, b_ref, o_ref, alpha=alpha)


def _rwkv_out_kernel(raw_ref, bonus_ref, gate_ref, lg_ref, lb_ref, bd_ref,
                     mq_ref, memkv_ref, x_ref, wout_ref, g_ref, b_ref, o_ref, *, alpha):
    raw = raw_ref[...]
    bd = bd_ref[...]
    inv_n = 1.0 / HEAD_DIM
    m = _split_dot(raw, bd, 3) * inv_n
    d = raw - m
    var = _split_dot(d * d, bd, 2) * inv_n
    normed = d * lax.rsqrt(var + LNX_EPS) * lg_ref[...] + lb_ref[...]
    mix = (normed + bonus_ref[...]) * gate_ref[...]
    _out_ln(mix.astype(BF16), mq_ref, memkv_ref, x_ref, wout_ref, g_ref, b_ref, o_ref, alpha=alpha)


def _mix_out_ln(mix_inputs, mq, mq_col, mem_kv, x, w_out, g, b, *, alpha, seq, tm, rwkv):
    t, d = x.shape
    tiles_per_seq = seq // tm
    row = lambda i: (i, 0)
    const = lambda i: (0, 0)
    c = d - MEM_WIDTH
    if rwkv:
        kern = _rwkv_out_kernel
        mix_specs = [pl.BlockSpec((tm, c), row)] * 3 + [pl.BlockSpec((1, c), const)] * 2 + [pl.BlockSpec((c, c), const)]
    else:
        kern = _sb_out_kernel
        mix_specs = [pl.BlockSpec((tm, c), row)]
    return pl.pallas_call(
        functools.partial(kern, alpha=alpha),
        grid=(t // tm,),
        in_specs=mix_specs + [
            pl.BlockSpec((tm, MEM_WIDTH), lambda i: (i, mq_col)),
            pl.BlockSpec((1,) + mem_kv.shape[1:], lambda i: (i // tiles_per_seq, 0, 0)),
            pl.BlockSpec((tm, d), row),
            pl.BlockSpec((d, d), const),
            pl.BlockSpec((1, d), const),
            pl.BlockSpec((1, d), const),
        ],
        out_specs=pl.BlockSpec((tm, d), row),
        out_shape=jax.ShapeDtypeStruct((t, d), F32),
        compiler_params=_cparams("parallel"),
        name="rwkv_out_ln" if rwkv else "sb_out_ln",
    )(*mix_inputs, mq, mem_kv, x, w_out, g, b)


def kernel(x, mem, ffn1_w_gate, ffn1_w_up, ffn1_w_down, ffn2_w_gate, ffn2_w_up, ffn2_w_down, ln_g, ln_b, w_out, w_mem_kv, rwkv_w_in, rwkv_mu, rwkv_w0, rwkv_w_up, rwkv_a0, rwkv_a_up, rwkv_g_up, rwkv_k_k, rwkv_k_a, rwkv_r_k, rwkv_lnx_g, rwkv_lnx_b, sb_w_in):
    bsz, seq, d = x.shape
    depth = ln_g.shape[0]
    c = d - MEM_WIDTH
    alpha = (2 * depth) ** 0.25
    n_mem = mem.shape[1]
    tm = min(512, seq)
    tf = 256

    head_of = jnp.arange(c) // HEAD_DIM
    bd = (head_of[:, None] == head_of[None, :]).astype(BF16)

    mem_kv = _proj(mem.reshape(bsz * n_mem, d), w_mem_kv.astype(BF16), jnp.ones((1, 2 * MEM_WIDTH), F32),
                   tm=256, out_dtype=BF16)
    mem_kv = mem_kv.reshape(bsz, n_mem, 2 * MEM_WIDTH)

    def ffn(xf, wg, wu, wd, g, b):
        return _ffn_ln(xf, wg.astype(BF16), wu.astype(BF16), wd.astype(BF16), g[None], b[None],
                       alpha=alpha, tm=tm, tf=tf)

    xf = x.reshape(bsz * seq, d)
    for i in range(depth):
        xf = ffn(xf, ffn1_w_gate[i], ffn1_w_up[i], ffn1_w_down[i], ln_g[i, 0], ln_b[i, 0])
        j = i // 2
        wo = w_out[i].astype(BF16)
        if i % 2 == 0:
            lora = jnp.zeros((LORA_WIDTH, 3 * c), F32)
            lora = lora.at[:DECAY_LORA, :c].set(rwkv_w_up[j])
            lora = lora.at[DECAY_LORA:DECAY_LORA + AAA_LORA, c:2 * c].set(rwkv_a_up[j])
            lora = lora.at[DECAY_LORA + AAA_LORA:, 2 * c:].set(rwkv_g_up[j])
            r, lw, k, v, kk, b, gate, bonus, mq = _rwkv_prep(
                xf, rwkv_w_in[j].astype(BF16), rwkv_mu[j][None], lora.astype(BF16),
                rwkv_w0[j][None], rwkv_a0[j][None], rwkv_k_k[j][None], rwkv_k_a[j][None],
                rwkv_r_k[j].reshape(1, c), bd, seq=seq, tm=256)
            sh = (bsz, seq, c)
            raw = _rwkv_chunk(r.reshape(sh), lw.reshape(sh), k.reshape(sh), v.reshape(sh),
                              kk.reshape(sh), b.reshape(sh), n_chunks=RWKV_CHUNKS_PER_STEP).reshape(bsz * seq, c)
            xf = _mix_out_ln((raw, bonus, gate, rwkv_lnx_g[j][None], rwkv_lnx_b[j][None], bd), mq, 0, mem_kv,
                             xf, wo, ln_g[i, 1][None], ln_b[i, 1][None], alpha=alpha, seq=seq, tm=tm, rwkv=True)
        else:
            q_scale = math.log2(math.e) / math.sqrt(HEAD_DIM)
            col_scale = jnp.where(jnp.arange(3 * c + MEM_WIDTH) < c, q_scale, 1.0).astype(F32)[None]
            p = _proj(xf, sb_w_in[j].astype(BF16), col_scale, tm=tm, out_dtype=BF16)
            mix = _sb_attention(p, bsz=bsz, seq=seq, c=c)
            xf = _mix_out_ln((mix,), p, (3 * c) // MEM_WIDTH, mem_kv, xf, wo, ln_g[i, 1][None], ln_b[i, 1][None],
                             alpha=alpha, seq=seq, tm=tm, rwkv=False)
        xf = ffn(xf, ffn2_w_gate[i], ffn2_w_up[i], ffn2_w_down[i], ln_g[i, 2], ln_b[i, 2])
    return xf.reshape(bsz, seq, d)
```

```python
import functools
import math

import jax
import jax.numpy as jnp
from jax import lax
from jax.experimental import pallas as pl
from jax.experimental.pallas import tpu as pltpu

F32 = jnp.float32
BF16 = jnp.bfloat16

HEAD_DIM = 64
LANES = 128
HEAD_SUM_TERMS = 3
N_MEM_HEADS = 4
MEM_WIDTH = N_MEM_HEADS * HEAD_DIM
DECAY_LORA = 64
AAA_LORA = 64
GATE_LORA = 128
LORA_WIDTH = DECAY_LORA + AAA_LORA + GATE_LORA
LN_EPS = 1e-5
LNX_EPS = 64e-5
RWKV_CHUNK = 64
RWKV_CHUNKS_PER_STEP = 4
INV_BLOCK = 16
MEM_ROWS = 128
SB_BLOCK = 256
SB_SUB = 128
SB_ROWS = 32
SB_GROUP = 2
SB_HEADS = 12
VMEM_LIMIT = 48 * 1024 * 1024


def _cparams(*sem):
    return pltpu.CompilerParams(dimension_semantics=sem, vmem_limit_bytes=VMEM_LIMIT)


def _dot(a, b):
    return jnp.dot(a, b, preferred_element_type=F32)


def _dot_nt(a, b):
    return lax.dot_general(a, b, (((1,), (1,)), ((), ())), preferred_element_type=F32)


def _dot_tn(a, b):
    return lax.dot_general(a, b, (((0,), (0,)), ((), ())), preferred_element_type=F32)


def _split_dot(x, ones, terms):
    acc = None
    rem = x
    for _ in range(terms):
        piece = rem.astype(BF16)
        part = _dot(piece, ones)
        acc = part if acc is None else acc + part
        rem = rem - piece.astype(F32)
    return acc


def _head_sum(x, sel_ref, back_ref, terms):
    s = _split_dot(x, sel_ref[...], terms)
    pieces = []
    for _ in range(terms):
        piece = s.astype(BF16)
        pieces.append(piece)
        s = s - piece.astype(F32)
    return _dot(jnp.concatenate(pieces, axis=1), back_ref[:terms * LANES, :])


def _softplus(x):
    return jnp.maximum(x, 0.0) + jnp.log(1.0 + jnp.exp(-jnp.abs(x)))


def _layer_norm(y, g, b):
    mu = jnp.mean(y, axis=-1, keepdims=True)
    d = y - mu
    var = jnp.mean(d * d, axis=-1, keepdims=True)
    return d * lax.rsqrt(var + LN_EPS) * g + b


def _ffn_kernel(x_ref, wg_ref, wu_ref, wd_ref, g_ref, b_ref, o_ref, h_ref, *, alpha, tf):
    x = x_ref[...]
    xb = x.astype(BF16)
    for j in range(wg_ref.shape[1] // tf):
        cols = slice(j * tf, (j + 1) * tf)
        gate = _dot(xb, wg_ref[:, cols])
        up = _dot(xb, wu_ref[:, cols])
        h_ref[:, cols] = (gate * jax.nn.sigmoid(gate) * up).astype(BF16)
    y = alpha * x + 0.5 * _dot(h_ref[...], wd_ref[...])
    o_ref[...] = _layer_norm(y, g_ref[...], b_ref[...])


def _resident(shape):
    return pl.BlockSpec(shape, lambda i: (0,) * len(shape), pipeline_mode=pl.Buffered(1))


def _ffn_ln(x, wg, wu, wd, g, b, *, alpha, tm, tf):
    t, d = x.shape
    f = wg.shape[1]
    return pl.pallas_call(
        functools.partial(_ffn_kernel, alpha=alpha, tf=tf),
        grid=(t // tm,),
        in_specs=[
            pl.BlockSpec((tm, d), lambda i: (i, 0)),
            _resident((d, f)), _resident((d, f)), _resident((f, d)),
            _resident((1, d)), _resident((1, d)),
        ],
        out_specs=pl.BlockSpec((tm, d), lambda i: (i, 0)),
        out_shape=jax.ShapeDtypeStruct((t, d), F32),
        scratch_shapes=[pltpu.VMEM((tm, f), BF16)],
        compiler_params=_cparams("parallel"),
        name="ffn_ln",
    )(x, wg, wu, wd, g, b)


def _proj_kernel(x_ref, w_ref, s_ref, o_ref):
    o_ref[...] = (_dot(x_ref[...].astype(BF16), w_ref[...]) * s_ref[...]).astype(o_ref.dtype)


def _proj(x, w, col_scale, *, tm, out_dtype):
    t, d = x.shape
    n = w.shape[1]
    return pl.pallas_call(
        _proj_kernel,
        grid=(t // tm,),
        in_specs=[pl.BlockSpec((tm, d), lambda i: (i, 0)), _resident((d, n)), _resident((1, n))],
        out_specs=pl.BlockSpec((tm, n), lambda i: (i, 0)),
        out_shape=jax.ShapeDtypeStruct((t, n), out_dtype),
        compiler_params=_cparams("parallel"),
        name="proj",
    )(x, w, col_scale)


def _rwkv_prep_kernel(x_ref, win_ref, mu_ref, lora_ref, w0_ref, a0_ref, kkw_ref, kaw_ref, rkw_ref, sel_ref, back_ref,
                      r_o, lw_o, k_o, v_o, kk_o, b_o, g_o, bonus_o, mq_o, carry_ref, *, tiles_per_seq, c):
    i = pl.program_id(0)
    shift = 3 * c + LORA_WIDTH

    @pl.when(i % tiles_per_seq == 0)
    def _():
        carry_ref[...] = jnp.zeros_like(carry_ref)

    p = _dot(x_ref[...].astype(BF16), win_ref[...])
    tm = p.shape[0]
    ps = p[:, :shift]
    mq_o[...] = p[:, shift:].astype(mq_o.dtype)

    row = lax.broadcasted_iota(jnp.int32, ps.shape, 0)
    prev = jnp.where(row == 0, carry_ref[...], pltpu.roll(ps, 1, 0))
    carry_ref[...] = ps[tm - 1:tm, :]
    z = ps + (prev - ps) * mu_ref[...]

    r = z[:, :c]
    kraw = z[:, c:2 * c]
    v = z[:, 2 * c:3 * c]
    lo = z[:, 3 * c:]
    lane = lax.broadcasted_iota(jnp.int32, lo.shape, 1)
    lhs = jnp.where(lane < DECAY_LORA, jnp.tanh(lo),
                    jnp.where(lane < DECAY_LORA + AAA_LORA, lo, jax.nn.sigmoid(lo)))
    pre = _dot(lhs.astype(BF16), lora_ref[...])
    w = -_softplus(-(pre[:, :c] + w0_ref[...])) - 0.5
    a = jax.nn.sigmoid(pre[:, c:2 * c] + a0_ref[...])
    g_o[...] = pre[:, 2 * c:]

    kkv = kraw * kkw_ref[...]
    ss = _head_sum(kkv * kkv, sel_ref, back_ref, 2)
    kkn = kkv * lax.rsqrt(jnp.maximum(ss, 1e-24))
    k2 = kraw * (1.0 + (a - 1.0) * kaw_ref[...])

    r_o[...] = r
    lw_o[...] = -jnp.exp(w)
    k_o[...] = k2
    v_o[...] = v
    kk_o[...] = kkn
    b_o[...] = kkn * a
    bonus_o[...] = _head_sum(r * k2 * rkw_ref[...], sel_ref, back_ref, 2) * v


def _rwkv_prep(x, w_in, mu, lora, w0, a0, k_k, k_a, r_k, sel, back, *, seq, tm):
    t, d = x.shape
    c = w0.shape[1]
    n_in = w_in.shape[1]
    shift = 3 * c + LORA_WIDTH
    row = lambda i: (i, 0)
    const = lambda i: (0, 0)
    wide = jax.ShapeDtypeStruct((t, c), F32)
    return pl.pallas_call(
        functools.partial(_rwkv_prep_kernel, tiles_per_seq=seq // tm, c=c),
        grid=(t // tm,),
        in_specs=[
            pl.BlockSpec((tm, d), row),
            pl.BlockSpec((d, n_in), const),
            pl.BlockSpec((1, shift), const),
            pl.BlockSpec((LORA_WIDTH, 3 * c), const),
            pl.BlockSpec((1, c), const), pl.BlockSpec((1, c), const), pl.BlockSpec((1, c), const),
            pl.BlockSpec((1, c), const), pl.BlockSpec((1, c), const),
            pl.BlockSpec(sel.shape, const), pl.BlockSpec(back.shape, const),
        ],
        out_specs=[pl.BlockSpec((tm, c), row)] * 8 + [pl.BlockSpec((tm, MEM_WIDTH), row)],
        out_shape=[wide] * 8 + [jax.ShapeDtypeStruct((t, MEM_WIDTH), BF16)],
        scratch_shapes=[pltpu.VMEM((1, shift), F32)],
        compiler_params=_cparams("arbitrary"),
        name="rwkv_prep",
    )(x, w_in, mu, lora, w0, a0, k_k, k_a, r_k, sel, back)


def _rwkv_chunk_kernel(r_ref, lw_ref, k_ref, v_ref, kk_ref, b_ref, o_ref, s_ref, *, n_pairs, n_chunks):
    @pl.when(pl.program_id(1) == 0)
    def _():
        s_ref[...] = jnp.zeros_like(s_ref)

    L = RWKV_CHUNK
    width = 2 * HEAD_DIM
    ti = lax.broadcasted_iota(jnp.int32, (L, width), 0)
    lane = lax.broadcasted_iota(jnp.int32, (L, width), 1)
    si = lane % HEAD_DIM
    first = lane < HEAD_DIM
    incl = si <= ti
    strict = si < ti
    eye = (si == ti).astype(F32)
    same_block = (si // INV_BLOCK) == (ti // INV_BLOCK)
    tri = lax.broadcasted_iota(jnp.int32, (L, L), 0) >= lax.broadcasted_iota(jnp.int32, (L, L), 1)
    tril_ones = tri.astype(BF16)

    def bdiag(x):
        return jnp.concatenate([jnp.where(first, x, 0.0), jnp.where(first, 0.0, x)], axis=0)

    def pdot(a, b):
        return _dot(a, bdiag(b))

    items = []
    for ci in range(n_chunks):
        rows = slice(ci * L, (ci + 1) * L)
        lw = lw_ref[0, rows, :]
        cum = _split_dot_left(tril_ones, lw, 3)
        cum_last = cum[L - 1:L, :]
        g_inv = jnp.exp(-cum)
        g_rel = jnp.exp(cum_last - cum)
        kk = kk_ref[0, rows, :]
        b = b_ref[0, rows, :]
        k = k_ref[0, rows, :]
        kap = kk * jnp.exp(cum - lw)
        rho = r_ref[0, rows, :] * jnp.exp(cum)
        bet = b * g_inv
        kt = k * g_inv
        bg = b * g_rel
        kg = k * g_rel
        v = v_ref[0, rows, :]
        g_last = jnp.exp(cum_last)
        for p in range(n_pairs):
            sl = slice(p * width, (p + 1) * width)
            items.append(dict(kap=kap[:, sl], rho=rho[:, sl], bet=bet[:, sl], kt=kt[:, sl], bg=bg[:, sl],
                              kg=kg[:, sl], v=v[:, sl], g_last=g_last[:, sl]))
    prod = [_dot_nt(jnp.concatenate([it["kap"], it["rho"]], axis=0),
                    jnp.concatenate([bdiag(it["bet"]), bdiag(it["kt"])], axis=0)) for it in items]
    m_ab = [jnp.where(strict, x[:L, :width], 0.0) for x in prod]
    m_ak = [jnp.where(strict, x[:L, width:], 0.0) for x in prod]
    a_r = [jnp.concatenate([jnp.where(incl, x[L:, :width], 0.0), jnp.where(incl, x[L:, width:], 0.0)], axis=1)
           for x in prod]

    md = [jnp.where(same_block, m, 0.0) for m in m_ab]
    nd = [m - d for m, d in zip(m_ab, md)]
    m2 = [pdot(d, d) for d in md]
    m4 = [pdot(x, x) for x in m2]
    m8 = [pdot(x, x) for x in m4]
    di = [eye - d for d in md]
    di = [x + pdot(x, q) for x, q in zip(di, m2)]
    di = [x + pdot(x, q) for x, q in zip(di, m4)]
    di = [x + pdot(x, q) for x, q in zip(di, m8)]
    xn = [pdot(x, y) for x, y in zip(di, nd)]
    x2 = [pdot(x, x) for x in xn]
    yi = [eye - x for x in xn]
    yi = [a + pdot(a, q) for a, q in zip(yi, x2)]
    t_inv = [pdot(a, x) for a, x in zip(yi, di)]

    bd_v = [bdiag(it["v"]) for it in items]
    mv = [_dot(m, x) for m, x in zip(m_ak, bd_v)]
    wu = [_dot(t, jnp.concatenate([bdiag(it["kap"]), bdiag(x)], axis=1))
          for t, it, x in zip(t_inv, items, mv)]

    state = [s_ref[p] for p in range(n_pairs)]
    for ci in range(n_chunks):
        idx = range(ci * n_pairs, (ci + 1) * n_pairs)
        ws = [_dot_nt(jnp.concatenate([wu[i][:, :width], items[i]["rho"]], axis=0), bdiag(s))
              for i, s in zip(idx, state)]
        u = [-wu[i][:, width:] - x[:L] for i, x in zip(idx, ws)]
        for n, (i, x, y) in enumerate(zip(idx, ws, u)):
            o_ref[0, ci * L:(ci + 1) * L, n * width:(n + 1) * width] = (
                x[L:] + _dot(a_r[i], jnp.concatenate([bdiag(y), bd_v[i]], axis=0)))
        upd = [_dot_tn(jnp.concatenate([y, items[i]["v"]], axis=0),
                       jnp.concatenate([items[i]["bg"], items[i]["kg"]], axis=0)) for i, y in zip(idx, u)]
        state = [s * items[i]["g_last"] + jnp.where(first, x[:L], x[L:]) for i, s, x in zip(idx, state, upd)]
    for p in range(n_pairs):
        s_ref[p] = state[p]


def _split_dot_left(ones, x, terms):
    acc = None
    rem = x
    for _ in range(terms):
        piece = rem.astype(BF16)
        part = _dot(ones, piece)
        acc = part if acc is None else acc + part
        rem = rem - piece.astype(F32)
    return acc


def _rwkv_chunk(r, lw, k, v, kk, b, *, n_chunks):
    bsz, seq, c = r.shape
    n_pairs = c // (2 * HEAD_DIM)
    rows = n_chunks * RWKV_CHUNK
    blk = pl.BlockSpec((1, rows, c), lambda bi, ci: (bi, ci, 0))
    return pl.pallas_call(
        functools.partial(_rwkv_chunk_kernel, n_pairs=n_pairs, n_chunks=n_chunks),
        grid=(bsz, seq // rows),
        in_specs=[blk] * 6,
        out_specs=blk,
        out_shape=jax.ShapeDtypeStruct((bsz, seq, c), F32),
        scratch_shapes=[pltpu.VMEM((n_pairs, HEAD_DIM, 2 * HEAD_DIM), F32)],
        compiler_params=_cparams("parallel", "arbitrary"),
        name="rwkv_chunk",
    )(r, lw, k, v, kk, b)


def _sb_kernel(q_ref, k_ref, v_ref, o_ref, y_ref, lhs_ref, out_ref, att_ref, carry_ref, acc_ref, *, n_heads):
    qi = pl.program_id(2)
    tq = q_ref.shape[0]
    sub = SB_SUB
    n_sub = tq // sub
    rc = SB_ROWS
    jr = lax.broadcasted_iota(jnp.int32, (2 * sub, 2 * sub), 0) % sub
    sc = lax.broadcasted_iota(jnp.int32, (2 * sub, 2 * sub), 1)
    sums = ((sc >= sub) | (jr >= sc)).astype(BF16)
    row_s = lax.broadcasted_iota(jnp.int32, (rc, sub), 0)
    col_s = lax.broadcasted_iota(jnp.int32, (rc, sub), 1)

    carry_ref[...] = jnp.zeros_like(carry_ref)
    acc_ref[...] = jnp.zeros_like(acc_ref)

    n_groups = n_heads // SB_GROUP

    def scores(g, rows, diagonal):
        for hg in range(SB_GROUP):
            h = g * SB_GROUP + hg
            sl = slice(h * HEAD_DIM, (h + 1) * HEAD_DIM)
            y_ref[h] = _dot_nt(q_ref[:, sl], k_ref[rows, sl])
            for r0 in range(0, tq, rc):
                for u in range(n_sub):
                    base = (hg * n_sub + u) * tq + r0
                    if diagonal and r0 + rc <= u * sub:
                        lhs_ref[g, base:base + rc, :] = jnp.zeros((rc, 2 * sub), BF16)
                        continue
                    y = y_ref[h, r0:r0 + rc, u * sub:(u + 1) * sub]
                    spu = jnp.maximum(y, 0.0) + jnp.log2(1.0 + jnp.exp2(-jnp.abs(y)))
                    if diagonal:
                        spu = jnp.where(col_s + u * sub < row_s + r0, spu, 0.0)
                    hi = spu.astype(BF16)
                    lo = (spu - hi.astype(F32)).astype(BF16)
                    lhs_ref[g, base:base + rc, :] = jnp.concatenate([hi, lo], axis=1)

    def suffix_sums(g):
        out_ref[g] = _dot(lhs_ref[g], sums)

    def weights(g, rows, diagonal):
        for hg in range(SB_GROUP):
            h = g * SB_GROUP + hg
            sl = slice(h * HEAD_DIM, (h + 1) * HEAD_DIM)
            for r0 in range(0, tq, rc):
                carry = carry_ref[h, r0:r0 + rc, :]
                for u in reversed(range(n_sub)):
                    if diagonal and r0 + rc <= u * sub:
                        att_ref[h, r0:r0 + rc, u * sub:(u + 1) * sub] = jnp.zeros((rc, sub), BF16)
                        continue
                    base = (hg * n_sub + u) * tq + r0
                    o_hu = out_ref[g, base:base + rc, :]
                    att = jnp.exp2(y_ref[h, r0:r0 + rc, u * sub:(u + 1) * sub] - o_hu[:, :sub] - carry)
                    if diagonal:
                        att = jnp.where(col_s + u * sub < row_s + r0, att, 0.0)
                    att_ref[h, r0:r0 + rc, u * sub:(u + 1) * sub] = att.astype(BF16)
                    carry = carry + o_hu[:, sub:]
                carry_ref[h, r0:r0 + rc, :] = carry
            acc_ref[h] += _dot(att_ref[h], v_ref[rows, sl])

    def key_block(j, diagonal):
        rows = pl.ds(pl.multiple_of(j * tq, tq), tq)
        for g in range(n_groups + 1):
            if g < n_groups:
                scores(g, rows, diagonal)
                suffix_sums(g)
            if g > 0:
                weights(g - 1, rows, diagonal)

    key_block(qi, True)

    def body(jj, _):
        key_block(qi - 1 - jj, False)
        return 0

    lax.fori_loop(0, qi, body, 0)
    o_ref[...] = jnp.concatenate([acc_ref[h] for h in range(n_heads)], axis=1).astype(o_ref.dtype)


def _sb_attention(p, *, bsz, seq, c):
    pair = SB_HEADS * HEAD_DIM
    n_pairs = c // pair
    nq = seq // SB_BLOCK
    n_sub = SB_BLOCK // SB_SUB
    return pl.pallas_call(
        functools.partial(_sb_kernel, n_heads=SB_HEADS),
        grid=(bsz, n_pairs, nq),
        in_specs=[
            pl.BlockSpec((SB_BLOCK, pair), lambda b, h, i: (b * nq + i, h)),
            pl.BlockSpec((seq, pair), lambda b, h, i: (b, n_pairs + h)),
            pl.BlockSpec((seq, pair), lambda b, h, i: (b, 2 * n_pairs + h)),
        ],
        out_specs=pl.BlockSpec((SB_BLOCK, pair), lambda b, h, i: (b * nq + i, h)),
        out_shape=jax.ShapeDtypeStruct((bsz * seq, c), BF16),
        scratch_shapes=[
            pltpu.VMEM((SB_HEADS, SB_BLOCK, SB_BLOCK), F32),
            pltpu.VMEM((SB_HEADS // SB_GROUP, SB_GROUP * n_sub * SB_BLOCK, 2 * SB_SUB), BF16),
            pltpu.VMEM((SB_HEADS // SB_GROUP, SB_GROUP * n_sub * SB_BLOCK, 2 * SB_SUB), F32),
            pltpu.VMEM((SB_HEADS, SB_BLOCK, SB_BLOCK), BF16),
            pltpu.VMEM((SB_HEADS, SB_BLOCK, SB_SUB), F32),
            pltpu.VMEM((SB_HEADS, SB_BLOCK, HEAD_DIM), F32),
        ],
        compiler_params=_cparams("parallel", "parallel", "arbitrary"),
        name="sb_attention",
    )(p, p, p)


def _memory_attention(mq, mem_kv):
    scale = 1.0 / math.sqrt(HEAD_DIM)
    tm = mq.shape[0]
    items = [(r, h) for r in range(0, tm, MEM_ROWS) for h in range(N_MEM_HEADS)]
    s = [_dot_nt(mq[r:r + MEM_ROWS, h * HEAD_DIM:(h + 1) * HEAD_DIM], mem_kv[:, h * HEAD_DIM:(h + 1) * HEAD_DIM])
         * scale for r, h in items]
    e = [jnp.exp(x - jnp.max(x, axis=-1, keepdims=True)).astype(BF16) for x in s]
    ones = jnp.ones((mem_kv.shape[0], HEAD_DIM), BF16)
    den = [_dot(x, ones) for x in e]
    o = [_dot(x, mem_kv[:, MEM_WIDTH + h * HEAD_DIM:MEM_WIDTH + (h + 1) * HEAD_DIM]) / d
         for x, d, (r, h) in zip(e, den, items)]
    chunks = [jnp.concatenate(o[i:i + N_MEM_HEADS], axis=1) for i in range(0, len(o), N_MEM_HEADS)]
    return jnp.concatenate(chunks, axis=0)


def _out_ln(mix_bf16, mq_ref, memkv_ref, x_ref, wout_ref, g_ref, b_ref, o_ref, *, alpha):
    c = mix_bf16.shape[1]
    mem_o = _memory_attention(mq_ref[...], memkv_ref[0])
    y = _dot(mix_bf16, wout_ref[:c, :]) + _dot(mem_o.astype(BF16), wout_ref[c:, :])
    o_ref[...] = _layer_norm(alpha * x_ref[...] + y, g_ref[...], b_ref[...])


def _sb_out_kernel(mix_ref, mq_ref, memkv_ref, x_ref, wout_ref, g_ref, b_ref, o_ref, *, alpha):
    _out_ln(mix_ref[...], mq_ref, memkv_ref, x_ref, wout_ref, g_ref, b_ref, o_ref, alpha=alpha)


def _rwkv_out_kernel(raw_ref, bonus_ref, gate_ref, lg_ref, lb_ref, sel_ref, back_ref,
                     mq_ref, memkv_ref, x_ref, wout_ref, g_ref, b_ref, o_ref, *, alpha):
    inv_n = 1.0 / HEAD_DIM
    mixes = []
    for r in range(0, raw_ref.shape[0], MEM_ROWS):
        rows = slice(r, r + MEM_ROWS)
        raw = raw_ref[rows, :]
        m = _head_sum(raw, sel_ref, back_ref, 3) * inv_n
        d = raw - m
        var = _head_sum(d * d, sel_ref, back_ref, 2) * inv_n
        normed = d * lax.rsqrt(var + LNX_EPS) * lg_ref[...] + lb_ref[...]
        mixes.append(((normed + bonus_ref[rows, :]) * gate_ref[rows, :]).astype(BF16))
    _out_ln(jnp.concatenate(mixes, axis=0), mq_ref, memkv_ref, x_ref, wout_ref, g_ref, b_ref, o_ref, alpha=alpha)


def _mix_out_ln(mix_inputs, mq, mq_col, mem_kv, x, w_out, g, b, *, alpha, seq, tm, rwkv):
    t, d = x.shape
    tiles_per_seq = seq // tm
    row = lambda i: (i, 0)
    const = lambda i: (0, 0)
    c = d - MEM_WIDTH
    if rwkv:
        kern = _rwkv_out_kernel
        mix_specs = ([pl.BlockSpec((tm, c), row)] * 3 + [pl.BlockSpec((1, c), const)] * 2
                     + [pl.BlockSpec(a.shape, const) for a in mix_inputs[5:]])
    else:
        kern = _sb_out_kernel
        mix_specs = [pl.BlockSpec((tm, c), row)]
    return pl.pallas_call(
        functools.partial(kern, alpha=alpha),
        grid=(t // tm,),
        in_specs=mix_specs + [
            pl.BlockSpec((tm, MEM_WIDTH), lambda i: (i, mq_col)),
            pl.BlockSpec((1,) + mem_kv.shape[1:], lambda i: (i // tiles_per_seq, 0, 0)),
            pl.BlockSpec((tm, d), row),
            pl.BlockSpec((d, d), const),
            pl.BlockSpec((1, d), const),
            pl.BlockSpec((1, d), const),
        ],
        out_specs=pl.BlockSpec((tm, d), row),
        out_shape=jax.ShapeDtypeStruct((t, d), F32),
        compiler_params=_cparams("parallel"),
        name="rwkv_out_ln" if rwkv else "sb_out_ln",
    )(*mix_inputs, mq, mem_kv, x, w_out, g, b)


def kernel(x, mem, ffn1_w_gate, ffn1_w_up, ffn1_w_down, ffn2_w_gate, ffn2_w_up, ffn2_w_down, ln_g, ln_b, w_out, w_mem_kv, rwkv_w_in, rwkv_mu, rwkv_w0, rwkv_w_up, rwkv_a0, rwkv_a_up, rwkv_g_up, rwkv_k_k, rwkv_k_a, rwkv_r_k, rwkv_lnx_g, rwkv_lnx_b, sb_w_in):
    bsz, seq, d = x.shape
    depth = ln_g.shape[0]
    c = d - MEM_WIDTH
    alpha = (2 * depth) ** 0.25
    n_mem = mem.shape[1]
    tm = min(512, seq)
    tf = 256

    head_of = jnp.arange(c) // HEAD_DIM
    sel = (head_of[:, None] == jnp.arange(LANES)[None, :]).astype(BF16)
    back = jnp.tile(sel.T, (HEAD_SUM_TERMS, 1))

    mem_kv = _proj(mem.reshape(bsz * n_mem, d), w_mem_kv.astype(BF16), jnp.ones((1, 2 * MEM_WIDTH), F32),
                   tm=256, out_dtype=BF16)
    mem_kv = mem_kv.reshape(bsz, n_mem, 2 * MEM_WIDTH)

    def ffn(xf, wg, wu, wd, g, b):
        return _ffn_ln(xf, wg.astype(BF16), wu.astype(BF16), wd.astype(BF16), g[None], b[None],
                       alpha=alpha, tm=tm, tf=tf)

    xf = x.reshape(bsz * seq, d)
    for i in range(depth):
        xf = ffn(xf, ffn1_w_gate[i], ffn1_w_up[i], ffn1_w_down[i], ln_g[i, 0], ln_b[i, 0])
        j = i // 2
        wo = w_out[i].astype(BF16)
        if i % 2 == 0:
            lora = jnp.zeros((LORA_WIDTH, 3 * c), F32)
            lora = lora.at[:DECAY_LORA, :c].set(rwkv_w_up[j])
            lora = lora.at[DECAY_LORA:DECAY_LORA + AAA_LORA, c:2 * c].set(rwkv_a_up[j])
            lora = lora.at[DECAY_LORA + AAA_LORA:, 2 * c:].set(rwkv_g_up[j])
            r, lw, k, v, kk, b, gate, bonus, mq = _rwkv_prep(
                xf, rwkv_w_in[j].astype(BF16), rwkv_mu[j][None], lora.astype(BF16),
                rwkv_w0[j][None], rwkv_a0[j][None], rwkv_k_k[j][None], rwkv_k_a[j][None],
                rwkv_r_k[j].reshape(1, c), sel, back, seq=seq, tm=256)
            sh = (bsz, seq, c)
            raw = _rwkv_chunk(r.reshape(sh), lw.reshape(sh), k.reshape(sh), v.reshape(sh),
                              kk.reshape(sh), b.reshape(sh), n_chunks=RWKV_CHUNKS_PER_STEP).reshape(bsz * seq, c)
            xf = _mix_out_ln((raw, bonus, gate, rwkv_lnx_g[j][None], rwkv_lnx_b[j][None], sel, back), mq, 0, mem_kv,
                             xf, wo, ln_g[i, 1][None], ln_b[i, 1][None], alpha=alpha, seq=seq, tm=tm, rwkv=True)
        else:
            q_scale = math.log2(math.e) / math.sqrt(HEAD_DIM)
            col_scale = jnp.where(jnp.arange(3 * c + MEM_WIDTH) < c, q_scale, 1.0).astype(F32)[None]
            p = _proj(xf, sb_w_in[j].astype(BF16), col_scale, tm=tm, out_dtype=BF16)
            mix = _sb_attention(p, bsz=bsz, seq=seq, c=c)
            xf = _mix_out_ln((mix,), p, (3 * c) // MEM_WIDTH, mem_kv, xf, wo, ln_g[i, 1][None], ln_b[i, 1][None],
                             alpha=alpha, seq=seq, tm=tm, rwkv=False)
        xf = ffn(xf, ffn2_w_gate[i], ffn2_w_up[i], ffn2_w_down[i], ln_g[i, 2], ln_b[i, 2])
    return xf.reshape(bsz, seq, d)
```

```python
import functools
import math

import jax
import jax.numpy as jnp
from jax import lax
from jax.experimental import pallas as pl
from jax.experimental.pallas import tpu as pltpu

F32 = jnp.float32
BF16 = jnp.bfloat16

HEAD_DIM = 64
LANES = 128
HEAD_SUM_TERMS = 3
N_MEM_HEADS = 4
MEM_WIDTH = N_MEM_HEADS * HEAD_DIM
DECAY_LORA = 64
AAA_LORA = 64
GATE_LORA = 128
LORA_WIDTH = DECAY_LORA + AAA_LORA + GATE_LORA
LN_EPS = 1e-5
LNX_EPS = 64e-5
RWKV_CHUNK = 64
RWKV_CHUNKS_PER_STEP = 4
INV_BLOCK = 16
MEM_ROWS = 128
SB_BLOCK = 256
SB_GROUP = 6
SB_HEADS = 12
VMEM_LIMIT = 48 * 1024 * 1024


def _cparams(*sem):
    return pltpu.CompilerParams(dimension_semantics=sem, vmem_limit_bytes=VMEM_LIMIT)


def _dot(a, b):
    return jnp.dot(a, b, preferred_element_type=F32)


def _dot_nt(a, b):
    return lax.dot_general(a, b, (((1,), (1,)), ((), ())), preferred_element_type=F32)


def _dot_tn(a, b):
    return lax.dot_general(a, b, (((0,), (0,)), ((), ())), preferred_element_type=F32)


def _split_dot(x, ones, terms):
    acc = None
    rem = x
    for _ in range(terms):
        piece = rem.astype(BF16)
        part = _dot(piece, ones)
        acc = part if acc is None else acc + part
        rem = rem - piece.astype(F32)
    return acc


def _head_sum(x, sel_ref, back_ref, terms):
    s = _split_dot(x, sel_ref[...], terms)
    pieces = []
    for _ in range(terms):
        piece = s.astype(BF16)
        pieces.append(piece)
        s = s - piece.astype(F32)
    return _dot(jnp.concatenate(pieces, axis=1), back_ref[:terms * LANES, :])


def _softplus(x):
    return jnp.maximum(x, 0.0) + jnp.log(1.0 + jnp.exp(-jnp.abs(x)))


def _layer_norm(y, g, b):
    mu = jnp.mean(y, axis=-1, keepdims=True)
    d = y - mu
    var = jnp.mean(d * d, axis=-1, keepdims=True)
    return d * lax.rsqrt(var + LN_EPS) * g + b


def _ffn_kernel(x_ref, wg_ref, wu_ref, wd_ref, g_ref, b_ref, o_ref, h_ref, *, alpha, tf):
    x = x_ref[...]
    xb = x.astype(BF16)
    for j in range(wg_ref.shape[1] // tf):
        cols = slice(j * tf, (j + 1) * tf)
        gate = _dot(xb, wg_ref[:, cols])
        up = _dot(xb, wu_ref[:, cols])
        h_ref[:, cols] = (gate * jax.nn.sigmoid(gate) * up).astype(BF16)
    y = alpha * x + 0.5 * _dot(h_ref[...], wd_ref[...])
    o_ref[...] = _layer_norm(y, g_ref[...], b_ref[...])


def _resident(shape):
    return pl.BlockSpec(shape, lambda i: (0,) * len(shape), pipeline_mode=pl.Buffered(1))


def _ffn_ln(x, wg, wu, wd, g, b, *, alpha, tm, tf):
    t, d = x.shape
    f = wg.shape[1]
    return pl.pallas_call(
        functools.partial(_ffn_kernel, alpha=alpha, tf=tf),
        grid=(t // tm,),
        in_specs=[
            pl.BlockSpec((tm, d), lambda i: (i, 0)),
            _resident((d, f)), _resident((d, f)), _resident((f, d)),
            _resident((1, d)), _resident((1, d)),
        ],
        out_specs=pl.BlockSpec((tm, d), lambda i: (i, 0)),
        out_shape=jax.ShapeDtypeStruct((t, d), F32),
        scratch_shapes=[pltpu.VMEM((tm, f), BF16)],
        compiler_params=_cparams("parallel"),
        name="ffn_ln",
    )(x, wg, wu, wd, g, b)


def _proj_kernel(x_ref, w_ref, s_ref, o_ref):
    o_ref[...] = (_dot(x_ref[...].astype(BF16), w_ref[...]) * s_ref[...]).astype(o_ref.dtype)


def _proj(x, w, col_scale, *, tm, out_dtype):
    t, d = x.shape
    n = w.shape[1]
    return pl.pallas_call(
        _proj_kernel,
        grid=(t // tm,),
        in_specs=[pl.BlockSpec((tm, d), lambda i: (i, 0)), _resident((d, n)), _resident((1, n))],
        out_specs=pl.BlockSpec((tm, n), lambda i: (i, 0)),
        out_shape=jax.ShapeDtypeStruct((t, n), out_dtype),
        compiler_params=_cparams("parallel"),
        name="proj",
    )(x, w, col_scale)


def _rwkv_prep_kernel(x_ref, win_ref, mu_ref, lora_ref, w0_ref, a0_ref, kkw_ref, kaw_ref, rkw_ref, sel_ref, back_ref,
                      r_o, lw_o, k_o, v_o, kk_o, b_o, g_o, bonus_o, mq_o, carry_ref, *, tiles_per_seq, c):
    i = pl.program_id(0)
    shift = 3 * c + LORA_WIDTH

    @pl.when(i % tiles_per_seq == 0)
    def _():
        carry_ref[...] = jnp.zeros_like(carry_ref)

    p = _dot(x_ref[...].astype(BF16), win_ref[...])
    tm = p.shape[0]
    ps = p[:, :shift]
    mq_o[...] = p[:, shift:].astype(mq_o.dtype)

    row = lax.broadcasted_iota(jnp.int32, ps.shape, 0)
    prev = jnp.where(row == 0, carry_ref[...], pltpu.roll(ps, 1, 0))
    carry_ref[...] = ps[tm - 1:tm, :]
    z = ps + (prev - ps) * mu_ref[...]

    r = z[:, :c]
    kraw = z[:, c:2 * c]
    v = z[:, 2 * c:3 * c]
    lo = z[:, 3 * c:]
    lane = lax.broadcasted_iota(jnp.int32, lo.shape, 1)
    lhs = jnp.where(lane < DECAY_LORA, jnp.tanh(lo),
                    jnp.where(lane < DECAY_LORA + AAA_LORA, lo, jax.nn.sigmoid(lo)))
    pre = _dot(lhs.astype(BF16), lora_ref[...])
    w = -_softplus(-(pre[:, :c] + w0_ref[...])) - 0.5
    a = jax.nn.sigmoid(pre[:, c:2 * c] + a0_ref[...])
    g_o[...] = pre[:, 2 * c:]

    kkv = kraw * kkw_ref[...]
    ss = _head_sum(kkv * kkv, sel_ref, back_ref, 2)
    kkn = kkv * lax.rsqrt(jnp.maximum(ss, 1e-24))
    k2 = kraw * (1.0 + (a - 1.0) * kaw_ref[...])

    r_o[...] = r
    lw_o[...] = -jnp.exp(w)
    k_o[...] = k2
    v_o[...] = v
    kk_o[...] = kkn
    b_o[...] = kkn * a
    bonus_o[...] = _head_sum(r * k2 * rkw_ref[...], sel_ref, back_ref, 2) * v


def _rwkv_prep(x, w_in, mu, lora, w0, a0, k_k, k_a, r_k, sel, back, *, seq, tm):
    t, d = x.shape
    c = w0.shape[1]
    n_in = w_in.shape[1]
    shift = 3 * c + LORA_WIDTH
    row = lambda i: (i, 0)
    const = lambda i: (0, 0)
    wide = jax.ShapeDtypeStruct((t, c), F32)
    return pl.pallas_call(
        functools.partial(_rwkv_prep_kernel, tiles_per_seq=seq // tm, c=c),
        grid=(t // tm,),
        in_specs=[
            pl.BlockSpec((tm, d), row),
            pl.BlockSpec((d, n_in), const),
            pl.BlockSpec((1, shift), const),
            pl.BlockSpec((LORA_WIDTH, 3 * c), const),
            pl.BlockSpec((1, c), const), pl.BlockSpec((1, c), const), pl.BlockSpec((1, c), const),
            pl.BlockSpec((1, c), const), pl.BlockSpec((1, c), const),
            pl.BlockSpec(sel.shape, const), pl.BlockSpec(back.shape, const),
        ],
        out_specs=[pl.BlockSpec((tm, c), row)] * 8 + [pl.BlockSpec((tm, MEM_WIDTH), row)],
        out_shape=[wide] * 8 + [jax.ShapeDtypeStruct((t, MEM_WIDTH), BF16)],
        scratch_shapes=[pltpu.VMEM((1, shift), F32)],
        compiler_params=_cparams("arbitrary"),
        name="rwkv_prep",
    )(x, w_in, mu, lora, w0, a0, k_k, k_a, r_k, sel, back)


def _rwkv_chunk_kernel(r_ref, lw_ref, k_ref, v_ref, kk_ref, b_ref, o_ref, s_ref, *, n_pairs, n_chunks):
    @pl.when(pl.program_id(1) == 0)
    def _():
        s_ref[...] = jnp.zeros_like(s_ref)

    L = RWKV_CHUNK
    width = 2 * HEAD_DIM
    ti = lax.broadcasted_iota(jnp.int32, (L, width), 0)
    lane = lax.broadcasted_iota(jnp.int32, (L, width), 1)
    si = lane % HEAD_DIM
    first = lane < HEAD_DIM
    incl = si <= ti
    strict = si < ti
    eye = (si == ti).astype(F32)
    same_block = (si // INV_BLOCK) == (ti // INV_BLOCK)
    tri = lax.broadcasted_iota(jnp.int32, (L, L), 0) >= lax.broadcasted_iota(jnp.int32, (L, L), 1)
    tril_ones = tri.astype(BF16)

    def bdiag(x):
        return jnp.concatenate([jnp.where(first, x, 0.0), jnp.where(first, 0.0, x)], axis=0)

    def pdot(a, b):
        return _dot(a, bdiag(b))

    items = []
    for ci in range(n_chunks):
        rows = slice(ci * L, (ci + 1) * L)
        lw = lw_ref[0, rows, :]
        cum = _split_dot_left(tril_ones, lw, 3)
        cum_last = cum[L - 1:L, :]
        g_inv = jnp.exp(-cum)
        g_rel = jnp.exp(cum_last - cum)
        kk = kk_ref[0, rows, :]
        b = b_ref[0, rows, :]
        k = k_ref[0, rows, :]
        kap = kk * jnp.exp(cum - lw)
        rho = r_ref[0, rows, :] * jnp.exp(cum)
        bet = b * g_inv
        kt = k * g_inv
        bg = b * g_rel
        kg = k * g_rel
        v = v_ref[0, rows, :]
        g_last = jnp.exp(cum_last)
        for p in range(n_pairs):
            sl = slice(p * width, (p + 1) * width)
            items.append(dict(kap=kap[:, sl], rho=rho[:, sl], bet=bet[:, sl], kt=kt[:, sl], bg=bg[:, sl],
                              kg=kg[:, sl], v=v[:, sl], g_last=g_last[:, sl]))
    prod = [_dot_nt(jnp.concatenate([it["kap"], it["rho"]], axis=0),
                    jnp.concatenate([bdiag(it["bet"]), bdiag(it["kt"])], axis=0)) for it in items]
    m_ab = [jnp.where(strict, x[:L, :width], 0.0) for x in prod]
    m_ak = [jnp.where(strict, x[:L, width:], 0.0) for x in prod]
    a_r = [jnp.concatenate([jnp.where(incl, x[L:, :width], 0.0), jnp.where(incl, x[L:, width:], 0.0)], axis=1)
           for x in prod]

    md = [jnp.where(same_block, m, 0.0) for m in m_ab]
    nd = [m - d for m, d in zip(m_ab, md)]
    m2 = [pdot(d, d) for d in md]
    m4 = [pdot(x, x) for x in m2]
    m8 = [pdot(x, x) for x in m4]
    di = [eye - d for d in md]
    di = [x + pdot(x, q) for x, q in zip(di, m2)]
    di = [x + pdot(x, q) for x, q in zip(di, m4)]
    di = [x + pdot(x, q) for x, q in zip(di, m8)]
    xn = [pdot(x, y) for x, y in zip(di, nd)]
    x2 = [pdot(x, x) for x in xn]
    yi = [eye - x for x in xn]
    yi = [a + pdot(a, q) for a, q in zip(yi, x2)]
    t_inv = [pdot(a, x) for a, x in zip(yi, di)]

    bd_v = [bdiag(it["v"]) for it in items]
    mv = [_dot(m, x) for m, x in zip(m_ak, bd_v)]
    wu = [_dot(t, jnp.concatenate([bdiag(it["kap"]), bdiag(x)], axis=1))
          for t, it, x in zip(t_inv, items, mv)]

    state = [s_ref[p] for p in range(n_pairs)]
    for ci in range(n_chunks):
        idx = range(ci * n_pairs, (ci + 1) * n_pairs)
        ws = [_dot_nt(jnp.concatenate([wu[i][:, :width], items[i]["rho"]], axis=0), bdiag(s))
              for i, s in zip(idx, state)]
        u = [-wu[i][:, width:] - x[:L] for i, x in zip(idx, ws)]
        for n, (i, x, y) in enumerate(zip(idx, ws, u)):
            o_ref[0, ci * L:(ci + 1) * L, n * width:(n + 1) * width] = (
                x[L:] + _dot(a_r[i], jnp.concatenate([bdiag(y), bd_v[i]], axis=0)))
        upd = [_dot_tn(jnp.concatenate([y, items[i]["v"]], axis=0),
                       jnp.concatenate([items[i]["bg"], items[i]["kg"]], axis=0)) for i, y in zip(idx, u)]
        state = [s * items[i]["g_last"] + jnp.where(first, x[:L], x[L:]) for i, s, x in zip(idx, state, upd)]
    for p in range(n_pairs):
        s_ref[p] = state[p]


def _split_dot_left(ones, x, terms):
    acc = None
    rem = x
    for _ in range(terms):
        piece = rem.astype(BF16)
        part = _dot(ones, piece)
        acc = part if acc is None else acc + part
        rem = rem - piece.astype(F32)
    return acc


def _rwkv_chunk(r, lw, k, v, kk, b, *, n_chunks):
    bsz, seq, c = r.shape
    n_pairs = c // (2 * HEAD_DIM)
    rows = n_chunks * RWKV_CHUNK
    blk = pl.BlockSpec((1, rows, c), lambda bi, ci: (bi, ci, 0))
    return pl.pallas_call(
        functools.partial(_rwkv_chunk_kernel, n_pairs=n_pairs, n_chunks=n_chunks),
        grid=(bsz, seq // rows),
        in_specs=[blk] * 6,
        out_specs=blk,
        out_shape=jax.ShapeDtypeStruct((bsz, seq, c), F32),
        scratch_shapes=[pltpu.VMEM((n_pairs, HEAD_DIM, 2 * HEAD_DIM), F32)],
        compiler_params=_cparams("parallel", "arbitrary"),
        name="rwkv_chunk",
    )(r, lw, k, v, kk, b)


def _sb_kernel(q_ref, k_ref, v_ref, o_ref, *, n_heads):
    qi = pl.program_id(2)
    tq = q_ref.shape[0]
    ti = lax.broadcasted_iota(jnp.int32, (tq, tq), 0)
    si = lax.broadcasted_iota(jnp.int32, (tq, tq), 1)
    before = si < ti
    suffix = (ti >= si).astype(BF16)
    q_all = q_ref[...]
    n_groups = n_heads // SB_GROUP

    def key_block(j, carries, accs, diagonal):
        rows = pl.ds(pl.multiple_of(j * tq, tq), tq)
        kb = k_ref[rows, :]
        vb = v_ref[rows, :]
        ys, sums = {}, {}
        new_carries, new_accs = list(carries), list(accs)

        def scores(g):
            sps = []
            for h in range(g * SB_GROUP, (g + 1) * SB_GROUP):
                sl = slice(h * HEAD_DIM, (h + 1) * HEAD_DIM)
                y = _dot_nt(q_all[:, sl], kb[:, sl])
                sp = jnp.maximum(y, 0.0) + jnp.log2(1.0 + jnp.exp2(-jnp.abs(y)))
                if diagonal:
                    sp = jnp.where(before, sp, 0.0)
                ys[h] = y
                sps.append(sp.astype(BF16))
            out = _dot(jnp.concatenate(sps, axis=0), suffix)
            for i, h in enumerate(range(g * SB_GROUP, (g + 1) * SB_GROUP)):
                sums[h] = out[i * tq:(i + 1) * tq, :]

        def weights(g):
            for h in range(g * SB_GROUP, (g + 1) * SB_GROUP):
                sl = slice(h * HEAD_DIM, (h + 1) * HEAD_DIM)
                att = jnp.exp2(ys[h] - sums[h] - carries[h])
                if diagonal:
                    att = jnp.where(before, att, 0.0)
                new_accs[h] = accs[h] + _dot(att.astype(BF16), vb[:, sl])
                new_carries[h] = carries[h] + sums[h][:, 0:1]

        for g in range(n_groups + 1):
            if g < n_groups:
                scores(g)
            if g > 0:
                weights(g - 1)
        return tuple(new_carries), tuple(new_accs)

    zeros_c = tuple(jnp.zeros((tq, 1), F32) for _ in range(n_heads))
    zeros_a = tuple(jnp.zeros((tq, HEAD_DIM), F32) for _ in range(n_heads))
    carries, accs = key_block(qi, zeros_c, zeros_a, True)

    def body(jj, state):
        return key_block(qi - 1 - jj, state[0], state[1], False)

    _, accs = lax.fori_loop(0, qi, body, (carries, accs))
    o_ref[...] = jnp.concatenate(accs, axis=1).astype(o_ref.dtype)


def _sb_attention(p, *, bsz, seq, c):
    pair = SB_HEADS * HEAD_DIM
    n_pairs = c // pair
    nq = seq // SB_BLOCK
    return pl.pallas_call(
        functools.partial(_sb_kernel, n_heads=SB_HEADS),
        grid=(bsz, n_pairs, nq),
        in_specs=[
            pl.BlockSpec((SB_BLOCK, pair), lambda b, h, i: (b * nq + i, h)),
            pl.BlockSpec((seq, pair), lambda b, h, i: (b, n_pairs + h)),
            pl.BlockSpec((seq, pair), lambda b, h, i: (b, 2 * n_pairs + h)),
        ],
        out_specs=pl.BlockSpec((SB_BLOCK, pair), lambda b, h, i: (b * nq + i, h)),
        out_shape=jax.ShapeDtypeStruct((bsz * seq, c), BF16),
        compiler_params=_cparams("parallel", "parallel", "arbitrary"),
        name="sb_attention",
    )(p, p, p)


def _memory_attention(mq, mem_kv):
    scale = 1.0 / math.sqrt(HEAD_DIM)
    tm = mq.shape[0]
    items = [(r, h) for r in range(0, tm, MEM_ROWS) for h in range(N_MEM_HEADS)]
    s = [_dot_nt(mq[r:r + MEM_ROWS, h * HEAD_DIM:(h + 1) * HEAD_DIM], mem_kv[:, h * HEAD_DIM:(h + 1) * HEAD_DIM])
         * scale for r, h in items]
    e = [jnp.exp(x - jnp.max(x, axis=-1, keepdims=True)).astype(BF16) for x in s]
    ones = jnp.ones((mem_kv.shape[0], HEAD_DIM), BF16)
    den = [_dot(x, ones) for x in e]
    o = [_dot(x, mem_kv[:, MEM_WIDTH + h * HEAD_DIM:MEM_WIDTH + (h + 1) * HEAD_DIM]) / d
         for x, d, (r, h) in zip(e, den, items)]
    chunks = [jnp.concatenate(o[i:i + N_MEM_HEADS], axis=1) for i in range(0, len(o), N_MEM_HEADS)]
    return jnp.concatenate(chunks, axis=0)


def _out_ln(mix_bf16, mq_ref, memkv_ref, x_ref, wout_ref, g_ref, b_ref, o_ref, *, alpha):
    c = mix_bf16.shape[1]
    mem_o = _memory_attention(mq_ref[...], memkv_ref[0])
    y = _dot(mix_bf16, wout_ref[:c, :]) + _dot(mem_o.astype(BF16), wout_ref[c:, :])
    o_ref[...] = _layer_norm(alpha * x_ref[...] + y, g_ref[...], b_ref[...])


def _sb_out_kernel(mix_ref, mq_ref, memkv_ref, x_ref, wout_ref, g_ref, b_ref, o_ref, *, alpha):
    _out_ln(mix_ref[...], mq_ref, memkv_ref, x_ref, wout_ref, g_ref, b_ref, o_ref, alpha=alpha)


def _rwkv_out_kernel(raw_ref, bonus_ref, gate_ref, lg_ref, lb_ref, sel_ref, back_ref,
                     mq_ref, memkv_ref, x_ref, wout_ref, g_ref, b_ref, o_ref, *, alpha):
    inv_n = 1.0 / HEAD_DIM
    mixes = []
    for r in range(0, raw_ref.shape[0], MEM_ROWS):
        rows = slice(r, r + MEM_ROWS)
        raw = raw_ref[rows, :]
        m = _head_sum(raw, sel_ref, back_ref, 3) * inv_n
        d = raw - m
        var = _head_sum(d * d, sel_ref, back_ref, 2) * inv_n
        normed = d * lax.rsqrt(var + LNX_EPS) * lg_ref[...] + lb_ref[...]
        mixes.append(((normed + bonus_ref[rows, :]) * gate_ref[rows, :]).astype(BF16))
    _out_ln(jnp.concatenate(mixes, axis=0), mq_ref, memkv_ref, x_ref, wout_ref, g_ref, b_ref, o_ref, alpha=alpha)


def _mix_out_ln(mix_inputs, mq, mq_col, mem_kv, x, w_out, g, b, *, alpha, seq, tm, rwkv):
    t, d = x.shape
    tiles_per_seq = seq // tm
    row = lambda i: (i, 0)
    const = lambda i: (0, 0)
    c = d - MEM_WIDTH
    if rwkv:
        kern = _rwkv_out_kernel
        mix_specs = ([pl.BlockSpec((tm, c), row)] * 3 + [pl.BlockSpec((1, c), const)] * 2
                     + [pl.BlockSpec(a.shape, const) for a in mix_inputs[5:]])
    else:
        kern = _sb_out_kernel
        mix_specs = [pl.BlockSpec((tm, c), row)]
    return pl.pallas_call(
        functools.partial(kern, alpha=alpha),
        grid=(t // tm,),
        in_specs=mix_specs + [
            pl.BlockSpec((tm, MEM_WIDTH), lambda i: (i, mq_col)),
            pl.BlockSpec((1,) + mem_kv.shape[1:], lambda i: (i // tiles_per_seq, 0, 0)),
            pl.BlockSpec((tm, d), row),
            pl.BlockSpec((d, d), const),
            pl.BlockSpec((1, d), const),
            pl.BlockSpec((1, d), const),
        ],
        out_specs=pl.BlockSpec((tm, d), row),
        out_shape=jax.ShapeDtypeStruct((t, d), F32),
        compiler_params=_cparams("parallel"),
        name="rwkv_out_ln" if rwkv else "sb_out_ln",
    )(*mix_inputs, mq, mem_kv, x, w_out, g, b)


def kernel(x, mem, ffn1_w_gate, ffn1_w_up, ffn1_w_down, ffn2_w_gate, ffn2_w_up, ffn2_w_down, ln_g, ln_b, w_out, w_mem_kv, rwkv_w_in, rwkv_mu, rwkv_w0, rwkv_w_up, rwkv_a0, rwkv_a_up, rwkv_g_up, rwkv_k_k, rwkv_k_a, rwkv_r_k, rwkv_lnx_g, rwkv_lnx_b, sb_w_in):
    bsz, seq, d = x.shape
    depth = ln_g.shape[0]
    c = d - MEM_WIDTH
    alpha = (2 * depth) ** 0.25
    n_mem = mem.shape[1]
    tm = min(512, seq)
    tf = 256

    head_of = jnp.arange(c) // HEAD_DIM
    sel = (head_of[:, None] == jnp.arange(LANES)[None, :]).astype(BF16)
    back = jnp.tile(sel.T, (HEAD_SUM_TERMS, 1))

    mem_kv = _proj(mem.reshape(bsz * n_mem, d), w_mem_kv.astype(BF16), jnp.ones((1, 2 * MEM_WIDTH), F32),
                   tm=256, out_dtype=BF16)
    mem_kv = mem_kv.reshape(bsz, n_mem, 2 * MEM_WIDTH)

    def ffn(xf, wg, wu, wd, g, b):
        return _ffn_ln(xf, wg.astype(BF16), wu.astype(BF16), wd.astype(BF16), g[None], b[None],
                       alpha=alpha, tm=tm, tf=tf)

    xf = x.reshape(bsz * seq, d)
    for i in range(depth):
        xf = ffn(xf, ffn1_w_gate[i], ffn1_w_up[i], ffn1_w_down[i], ln_g[i, 0], ln_b[i, 0])
        j = i // 2
        wo = w_out[i].astype(BF16)
        if i % 2 == 0:
            lora = jnp.zeros((LORA_WIDTH, 3 * c), F32)
            lora = lora.at[:DECAY_LORA, :c].set(rwkv_w_up[j])
            lora = lora.at[DECAY_LORA:DECAY_LORA + AAA_LORA, c:2 * c].set(rwkv_a_up[j])
            lora = lora.at[DECAY_LORA + AAA_LORA:, 2 * c:].set(rwkv_g_up[j])
            r, lw, k, v, kk, b, gate, bonus, mq = _rwkv_prep(
                xf, rwkv_w_in[j].astype(BF16), rwkv_mu[j][None], lora.astype(BF16),
                rwkv_w0[j][None], rwkv_a0[j][None], rwkv_k_k[j][None], rwkv_k_a[j][None],
                rwkv_r_k[j].reshape(1, c), sel, back, seq=seq, tm=256)
            sh = (bsz, seq, c)
            raw = _rwkv_chunk(r.reshape(sh), lw.reshape(sh), k.reshape(sh), v.reshape(sh),
                              kk.reshape(sh), b.reshape(sh), n_chunks=RWKV_CHUNKS_PER_STEP).reshape(bsz * seq, c)
            xf = _mix_out_ln((raw, bonus, gate, rwkv_lnx_g[j][None], rwkv_lnx_b[j][None], sel, back), mq, 0, mem_kv,
                             xf, wo, ln_g[i, 1][None], ln_b[i, 1][None], alpha=alpha, seq=seq, tm=tm, rwkv=True)
        else:
            q_scale = math.log2(math.e) / math.sqrt(HEAD_DIM)
            col_scale = jnp.where(jnp.arange(3 * c + MEM_WIDTH) < c, q_scale, 1.0).astype(F32)[None]
            p = _proj(xf, sb_w_in[j].astype(BF16), col_scale, tm=tm, out_dtype=BF16)
            mix = _sb_attention(p, bsz=bsz, seq=seq, c=c)
            xf = _mix_out_ln((mix,), p, (3 * c) // MEM_WIDTH, mem_kv, xf, wo, ln_g[i, 1][None], ln_b[i, 1][None],
                             alpha=alpha, seq=seq, tm=tm, rwkv=False)
        xf = ffn(xf, ffn2_w_gate[i], ffn2_w_up[i], ffn2_w_down[i], ln_g[i, 2], ln_b[i, 2])
    return xf.reshape(bsz, seq, d)
```

```python
import functools
import math

import jax
import jax.numpy as jnp
from jax import lax
from jax.experimental import pallas as pl
from jax.experimental.pallas import tpu as pltpu

F32 = jnp.float32
BF16 = jnp.bfloat16

HEAD_DIM = 64
LANES = 128
SUBLANES = 8
HEAD_SUM_TERMS = 3
N_MEM_HEADS = 4
MEM_WIDTH = N_MEM_HEADS * HEAD_DIM
DECAY_LORA = 64
AAA_LORA = 64
GATE_LORA = 128
LORA_WIDTH = DECAY_LORA + AAA_LORA + GATE_LORA
LN_EPS = 1e-5
LNX_EPS = 64e-5
RWKV_CHUNK = 64
RWKV_CHUNKS_PER_STEP = 4
INV_BLOCK = 16
MEM_ROWS = 128
SB_BLOCK = 256
SB_GROUP = 6
SB_HEADS = 12
SB_TERMS = 2
SB_DEAD_LOG2 = 192.0
SB_BOUND_MARGIN = 1.05
VMEM_LIMIT = 48 * 1024 * 1024


def _cparams(*sem):
    return pltpu.CompilerParams(dimension_semantics=sem, vmem_limit_bytes=VMEM_LIMIT)


def _dot(a, b):
    return jnp.dot(a, b, preferred_element_type=F32)


def _dot_nt(a, b):
    return lax.dot_general(a, b, (((1,), (1,)), ((), ())), preferred_element_type=F32)


def _dot_tn(a, b):
    return lax.dot_general(a, b, (((0,), (0,)), ((), ())), preferred_element_type=F32)


def _split_dot(x, ones, terms):
    acc = None
    rem = x
    for _ in range(terms):
        piece = rem.astype(BF16)
        part = _dot(piece, ones)
        acc = part if acc is None else acc + part
        rem = rem - piece.astype(F32)
    return acc


def _head_sum(x, sel_ref, back_ref, terms):
    s = _split_dot(x, sel_ref[...], terms)
    pieces = []
    for _ in range(terms):
        piece = s.astype(BF16)
        pieces.append(piece)
        s = s - piece.astype(F32)
    return _dot(jnp.concatenate(pieces, axis=1), back_ref[:terms * LANES, :])


def _softplus(x):
    return jnp.maximum(x, 0.0) + jnp.log(1.0 + jnp.exp(-jnp.abs(x)))


def _layer_norm(y, g, b):
    mu = jnp.mean(y, axis=-1, keepdims=True)
    d = y - mu
    var = jnp.mean(d * d, axis=-1, keepdims=True)
    return d * lax.rsqrt(var + LN_EPS) * g + b


def _ffn_kernel(x_ref, wg_ref, wu_ref, wd_ref, g_ref, b_ref, o_ref, h_ref, *, alpha, tf):
    x = x_ref[...]
    xb = x.astype(BF16)
    for j in range(wg_ref.shape[1] // tf):
        cols = slice(j * tf, (j + 1) * tf)
        gate = _dot(xb, wg_ref[:, cols])
        up = _dot(xb, wu_ref[:, cols])
        h_ref[:, cols] = (gate * jax.nn.sigmoid(gate) * up).astype(BF16)
    y = alpha * x + 0.5 * _dot(h_ref[...], wd_ref[...])
    o_ref[...] = _layer_norm(y, g_ref[...], b_ref[...])


def _resident(shape):
    return pl.BlockSpec(shape, lambda i: (0,) * len(shape), pipeline_mode=pl.Buffered(1))


def _ffn_ln(x, wg, wu, wd, g, b, *, alpha, tm, tf):
    t, d = x.shape
    f = wg.shape[1]
    return pl.pallas_call(
        functools.partial(_ffn_kernel, alpha=alpha, tf=tf),
        grid=(t // tm,),
        in_specs=[
            pl.BlockSpec((tm, d), lambda i: (i, 0)),
            _resident((d, f)), _resident((d, f)), _resident((f, d)),
            _resident((1, d)), _resident((1, d)),
        ],
        out_specs=pl.BlockSpec((tm, d), lambda i: (i, 0)),
        out_shape=jax.ShapeDtypeStruct((t, d), F32),
        scratch_shapes=[pltpu.VMEM((tm, f), BF16)],
        compiler_params=_cparams("parallel"),
        name="ffn_ln",
    )(x, wg, wu, wd, g, b)


def _proj_kernel(x_ref, w_ref, s_ref, o_ref):
    o_ref[...] = (_dot(x_ref[...].astype(BF16), w_ref[...]) * s_ref[...]).astype(o_ref.dtype)


def _proj(x, w, col_scale, *, tm, out_dtype):
    t, d = x.shape
    n = w.shape[1]
    return pl.pallas_call(
        _proj_kernel,
        grid=(t // tm,),
        in_specs=[pl.BlockSpec((tm, d), lambda i: (i, 0)), _resident((d, n)), _resident((1, n))],
        out_specs=pl.BlockSpec((tm, n), lambda i: (i, 0)),
        out_shape=jax.ShapeDtypeStruct((t, n), out_dtype),
        compiler_params=_cparams("parallel"),
        name="proj",
    )(x, w, col_scale)


def _rwkv_prep_kernel(x_ref, win_ref, mu_ref, lora_ref, w0_ref, a0_ref, kkw_ref, kaw_ref, rkw_ref, sel_ref, back_ref,
                      r_o, lw_o, k_o, v_o, kk_o, b_o, g_o, bonus_o, mq_o, carry_ref, *, tiles_per_seq, c):
    i = pl.program_id(0)
    shift = 3 * c + LORA_WIDTH

    @pl.when(i % tiles_per_seq == 0)
    def _():
        carry_ref[...] = jnp.zeros_like(carry_ref)

    p = _dot(x_ref[...].astype(BF16), win_ref[...])
    tm = p.shape[0]
    ps = p[:, :shift]
    mq_o[...] = p[:, shift:].astype(mq_o.dtype)

    row = lax.broadcasted_iota(jnp.int32, ps.shape, 0)
    prev = jnp.where(row == 0, carry_ref[...], pltpu.roll(ps, 1, 0))
    carry_ref[...] = ps[tm - 1:tm, :]
    z = ps + (prev - ps) * mu_ref[...]

    r = z[:, :c]
    kraw = z[:, c:2 * c]
    v = z[:, 2 * c:3 * c]
    lo = z[:, 3 * c:]
    lane = lax.broadcasted_iota(jnp.int32, lo.shape, 1)
    lhs = jnp.where(lane < DECAY_LORA, jnp.tanh(lo),
                    jnp.where(lane < DECAY_LORA + AAA_LORA, lo, jax.nn.sigmoid(lo)))
    pre = _dot(lhs.astype(BF16), lora_ref[...])
    w = -_softplus(-(pre[:, :c] + w0_ref[...])) - 0.5
    a = jax.nn.sigmoid(pre[:, c:2 * c] + a0_ref[...])
    g_o[...] = pre[:, 2 * c:]

    kkv = kraw * kkw_ref[...]
    ss = _head_sum(kkv * kkv, sel_ref, back_ref, 2)
    kkn = kkv * lax.rsqrt(jnp.maximum(ss, 1e-24))
    k2 = kraw * (1.0 + (a - 1.0) * kaw_ref[...])

    r_o[...] = r
    lw_o[...] = -jnp.exp(w)
    k_o[...] = k2
    v_o[...] = v
    kk_o[...] = kkn
    b_o[...] = kkn * a
    bonus_o[...] = _head_sum(r * k2 * rkw_ref[...], sel_ref, back_ref, 2) * v


def _rwkv_prep(x, w_in, mu, lora, w0, a0, k_k, k_a, r_k, sel, back, *, seq, tm):
    t, d = x.shape
    c = w0.shape[1]
    n_in = w_in.shape[1]
    shift = 3 * c + LORA_WIDTH
    row = lambda i: (i, 0)
    const = lambda i: (0, 0)
    wide = jax.ShapeDtypeStruct((t, c), F32)
    return pl.pallas_call(
        functools.partial(_rwkv_prep_kernel, tiles_per_seq=seq // tm, c=c),
        grid=(t // tm,),
        in_specs=[
            pl.BlockSpec((tm, d), row),
            pl.BlockSpec((d, n_in), const),
            pl.BlockSpec((1, shift), const),
            pl.BlockSpec((LORA_WIDTH, 3 * c), const),
            pl.BlockSpec((1, c), const), pl.BlockSpec((1, c), const), pl.BlockSpec((1, c), const),
            pl.BlockSpec((1, c), const), pl.BlockSpec((1, c), const),
            pl.BlockSpec(sel.shape, const), pl.BlockSpec(back.shape, const),
        ],
        out_specs=[pl.BlockSpec((tm, c), row)] * 8 + [pl.BlockSpec((tm, MEM_WIDTH), row)],
        out_shape=[wide] * 8 + [jax.ShapeDtypeStruct((t, MEM_WIDTH), BF16)],
        scratch_shapes=[pltpu.VMEM((1, shift), F32)],
        compiler_params=_cparams("arbitrary"),
        name="rwkv_prep",
    )(x, w_in, mu, lora, w0, a0, k_k, k_a, r_k, sel, back)


def _rwkv_chunk_kernel(r_ref, lw_ref, k_ref, v_ref, kk_ref, b_ref, o_ref, s_ref, *, n_pairs, n_chunks):
    @pl.when(pl.program_id(1) == 0)
    def _():
        s_ref[...] = jnp.zeros_like(s_ref)

    L = RWKV_CHUNK
    width = 2 * HEAD_DIM
    ti = lax.broadcasted_iota(jnp.int32, (L, width), 0)
    lane = lax.broadcasted_iota(jnp.int32, (L, width), 1)
    si = lane % HEAD_DIM
    first = lane < HEAD_DIM
    incl = si <= ti
    strict = si < ti
    eye = (si == ti).astype(F32)
    same_block = (si // INV_BLOCK) == (ti // INV_BLOCK)
    tri = lax.broadcasted_iota(jnp.int32, (L, L), 0) >= lax.broadcasted_iota(jnp.int32, (L, L), 1)
    tril_ones = tri.astype(BF16)

    def bdiag(x):
        x = x.astype(BF16)
        zero = jnp.zeros_like(x)
        return jnp.concatenate([jnp.where(first, x, zero), jnp.where(first, zero, x)], axis=0)

    def pdot(a, b):
        return _dot(a.astype(BF16), bdiag(b))

    items = []
    for ci in range(n_chunks):
        rows = slice(ci * L, (ci + 1) * L)
        lw = lw_ref[0, rows, :]
        cum = _split_dot_left(tril_ones, lw, 3)
        cum_last = cum[L - 1:L, :]
        g_inv = jnp.exp(-cum)
        g_rel = jnp.exp(cum_last - cum)
        kk = kk_ref[0, rows, :]
        b = b_ref[0, rows, :]
        k = k_ref[0, rows, :]
        kap = kk * jnp.exp(cum - lw)
        rho = r_ref[0, rows, :] * jnp.exp(cum)
        bet = b * g_inv
        kt = k * g_inv
        bg = b * g_rel
        kg = k * g_rel
        v = v_ref[0, rows, :]
        g_last = jnp.exp(cum_last)
        for p in range(n_pairs):
            sl = slice(p * width, (p + 1) * width)
            items.append(dict(kap=kap[:, sl], rho=rho[:, sl], bet=bet[:, sl], kt=kt[:, sl], bg=bg[:, sl],
                              kg=kg[:, sl], v=v[:, sl], g_last=g_last[:, sl]))
    prod = [_dot_nt(jnp.concatenate([it["kap"], it["rho"]], axis=0).astype(BF16),
                    jnp.concatenate([bdiag(it["bet"]), bdiag(it["kt"])], axis=0)) for it in items]
    m_ab = [jnp.where(strict, x[:L, :width], 0.0) for x in prod]
    m_ak = [jnp.where(strict, x[:L, width:], 0.0) for x in prod]
    a_r = [jnp.concatenate([jnp.where(incl, x[L:, :width], 0.0), jnp.where(incl, x[L:, width:], 0.0)],
                           axis=1).astype(BF16) for x in prod]

    md = [jnp.where(same_block, m, 0.0) for m in m_ab]
    nd = [m - d for m, d in zip(m_ab, md)]
    m2 = [pdot(d, d) for d in md]
    m4 = [pdot(x, x) for x in m2]
    m8 = [pdot(x, x) for x in m4]
    di = [eye - d for d in md]
    di = [x + pdot(x, q) for x, q in zip(di, m2)]
    di = [x + pdot(x, q) for x, q in zip(di, m4)]
    di = [x + pdot(x, q) for x, q in zip(di, m8)]
    xn = [pdot(x, y) for x, y in zip(di, nd)]
    x2 = [pdot(x, x) for x in xn]
    yi = [eye - x for x in xn]
    yi = [a + pdot(a, q) for a, q in zip(yi, x2)]
    t_inv = [pdot(a, x) for a, x in zip(yi, di)]

    bd_v = [bdiag(it["v"]) for it in items]
    mv = [_dot(m.astype(BF16), x) for m, x in zip(m_ak, bd_v)]
    wu = [_dot(t.astype(BF16), jnp.concatenate([bdiag(it["kap"]), bdiag(x)], axis=1))
          for t, it, x in zip(t_inv, items, mv)]

    state = [s_ref[p] for p in range(n_pairs)]
    for ci in range(n_chunks):
        idx = range(ci * n_pairs, (ci + 1) * n_pairs)
        ws = [_dot_nt(jnp.concatenate([wu[i][:, :width], items[i]["rho"]], axis=0).astype(BF16), bdiag(s))
              for i, s in zip(idx, state)]
        u = [-wu[i][:, width:] - x[:L] for i, x in zip(idx, ws)]
        for n, (i, x, y) in enumerate(zip(idx, ws, u)):
            o_ref[0, ci * L:(ci + 1) * L, n * width:(n + 1) * width] = (
                x[L:] + _dot(a_r[i], jnp.concatenate([bdiag(y), bd_v[i]], axis=0)))
        upd = [_dot_tn(jnp.concatenate([y, items[i]["v"]], axis=0).astype(BF16),
                       jnp.concatenate([items[i]["bg"], items[i]["kg"]], axis=0).astype(BF16))
               for i, y in zip(idx, u)]
        state = [s * items[i]["g_last"] + jnp.where(first, x[:L], x[L:]) for i, s, x in zip(idx, state, upd)]
    for p in range(n_pairs):
        s_ref[p] = state[p]


def _split_dot_left(ones, x, terms):
    acc = None
    rem = x
    for _ in range(terms):
        piece = rem.astype(BF16)
        part = _dot(ones, piece)
        acc = part if acc is None else acc + part
        rem = rem - piece.astype(F32)
    return acc


def _rwkv_chunk(r, lw, k, v, kk, b, *, n_chunks):
    bsz, seq, c = r.shape
    n_pairs = c // (2 * HEAD_DIM)
    rows = n_chunks * RWKV_CHUNK
    blk = pl.BlockSpec((1, rows, c), lambda bi, ci: (bi, ci, 0))
    return pl.pallas_call(
        functools.partial(_rwkv_chunk_kernel, n_pairs=n_pairs, n_chunks=n_chunks),
        grid=(bsz, seq // rows),
        in_specs=[blk] * 6,
        out_specs=blk,
        out_shape=jax.ShapeDtypeStruct((bsz, seq, c), F32),
        scratch_shapes=[pltpu.VMEM((n_pairs, HEAD_DIM, 2 * HEAD_DIM), F32)],
        compiler_params=_cparams("parallel", "arbitrary"),
        name="rwkv_chunk",
    )(r, lw, k, v, kk, b)


def _max_sq_norm(x):
    width = x.shape[1]
    chan = lax.broadcasted_iota(jnp.int32, (width, LANES), 0) // HEAD_DIM
    sel = (chan == lax.broadcasted_iota(jnp.int32, (width, LANES), 1)).astype(BF16)
    xf = x.astype(F32)
    return jnp.max(_dot((xf * xf).astype(BF16), sel), keepdims=True)


def _sb_proj_kernel(x_ref, w_ref, s_ref, o_ref, ksq_ref, *, c):
    p = (_dot(x_ref[...].astype(BF16), w_ref[...]) * s_ref[...]).astype(o_ref.dtype)
    o_ref[...] = p
    ksq_ref[...] = jnp.broadcast_to(_max_sq_norm(p[:, c:2 * c]), ksq_ref.shape)


def _sb_proj(x, w, col_scale, *, c, tm):
    t, d = x.shape
    n = w.shape[1]
    return pl.pallas_call(
        functools.partial(_sb_proj_kernel, c=c),
        grid=(t // tm,),
        in_specs=[pl.BlockSpec((tm, d), lambda i: (i, 0)), _resident((d, n)), _resident((1, n))],
        out_specs=[pl.BlockSpec((tm, n), lambda i: (i, 0)), pl.BlockSpec((SUBLANES, LANES), lambda i: (i, 0))],
        out_shape=[jax.ShapeDtypeStruct((t, n), BF16),
                   jax.ShapeDtypeStruct((t // tm * SUBLANES, LANES), F32)],
        compiler_params=_cparams("parallel"),
        name="sb_proj",
    )(x, w, col_scale)


def _sb_kernel(q_ref, k_ref, v_ref, ksq_ref, o_ref, *, n_heads):
    qi = pl.program_id(2)
    tq = q_ref.shape[0]
    ti = lax.broadcasted_iota(jnp.int32, (tq, tq), 0)
    si = lax.broadcasted_iota(jnp.int32, (tq, tq), 1)
    before = si < ti
    suffix = jnp.concatenate([(ti >= si).astype(BF16)] * SB_TERMS, axis=0)
    q_all = q_ref[...]
    n_groups = n_heads // SB_GROUP

    def key_block(j, carries, accs, diagonal):
        rows = pl.ds(pl.multiple_of(j * tq, tq), tq)
        kb = k_ref[rows, :]
        vb = v_ref[rows, :]
        ys, sums = {}, {}
        new_carries, new_accs = list(carries), list(accs)

        def scores(g):
            sps = []
            for h in range(g * SB_GROUP, (g + 1) * SB_GROUP):
                sl = slice(h * HEAD_DIM, (h + 1) * HEAD_DIM)
                y = _dot_nt(q_all[:, sl], kb[:, sl])
                sp = jnp.maximum(y, 0.0) + jnp.log2(1.0 + jnp.exp2(-jnp.abs(y)))
                if diagonal:
                    sp = jnp.where(before, sp, 0.0)
                ys[h] = y
                pieces = []
                for _ in range(SB_TERMS):
                    piece = sp.astype(BF16)
                    pieces.append(piece)
                    sp = sp - piece.astype(F32)
                sps.append(jnp.concatenate(pieces, axis=1))
            out = _dot(jnp.concatenate(sps, axis=0), suffix)
            for i, h in enumerate(range(g * SB_GROUP, (g + 1) * SB_GROUP)):
                sums[h] = out[i * tq:(i + 1) * tq, :]

        def weights(g):
            for h in range(g * SB_GROUP, (g + 1) * SB_GROUP):
                sl = slice(h * HEAD_DIM, (h + 1) * HEAD_DIM)
                att = jnp.exp2(ys[h] - sums[h] - carries[h])
                if diagonal:
                    att = jnp.where(before, att, 0.0)
                new_accs[h] = accs[h] + _dot(att.astype(BF16), vb[:, sl])
                new_carries[h] = carries[h] + sums[h][:, 0:1]

        for g in range(n_groups + 1):
            if g < n_groups:
                scores(g)
            if g > 0:
                weights(g - 1)
        return tuple(new_carries), tuple(new_accs)

    zeros_c = tuple(jnp.zeros((tq, 1), F32) for _ in range(n_heads))
    zeros_a = tuple(jnp.zeros((tq, HEAD_DIM), F32) for _ in range(n_heads))
    carries, accs = key_block(qi, zeros_c, zeros_a, True)

    bound = jnp.sqrt(_max_sq_norm(q_all) * ksq_ref[0, 0:1, 0:1]) * SB_BOUND_MARGIN

    def alive(state):
        jj, carries, _ = state
        slack = functools.reduce(jnp.minimum, [jnp.min(c, keepdims=True) for c in carries]) - bound
        return jnp.logical_and(jj < qi, slack[0, 0] < SB_DEAD_LOG2)

    def body(state):
        jj, carries, accs = state
        carries, accs = key_block(qi - 1 - jj, carries, accs, False)
        return jj + 1, carries, accs

    _, _, accs = lax.while_loop(alive, body, (jnp.int32(0), carries, accs))
    o_ref[...] = jnp.concatenate(accs, axis=1).astype(o_ref.dtype)


def _sb_attention(p, ksq, *, bsz, seq, c):
    pair = SB_HEADS * HEAD_DIM
    n_pairs = c // pair
    nq = seq // SB_BLOCK
    return pl.pallas_call(
        functools.partial(_sb_kernel, n_heads=SB_HEADS),
        grid=(bsz, n_pairs, nq),
        in_specs=[
            pl.BlockSpec((SB_BLOCK, pair), lambda b, h, i: (b * nq + i, h)),
            pl.BlockSpec((seq, pair), lambda b, h, i: (b, n_pairs + h)),
            pl.BlockSpec((seq, pair), lambda b, h, i: (b, 2 * n_pairs + h)),
            pl.BlockSpec((1, SUBLANES, LANES), lambda b, h, i: (b, 0, 0)),
        ],
        out_specs=pl.BlockSpec((SB_BLOCK, pair), lambda b, h, i: (b * nq + i, h)),
        out_shape=jax.ShapeDtypeStruct((bsz * seq, c), BF16),
        compiler_params=_cparams("parallel", "parallel", "arbitrary"),
        name="sb_attention",
    )(p, p, p, ksq)


def _memory_attention(mq, mem_kv):
    scale = 1.0 / math.sqrt(HEAD_DIM)
    tm = mq.shape[0]
    items = [(r, h) for r in range(0, tm, MEM_ROWS) for h in range(N_MEM_HEADS)]
    s = [_dot_nt(mq[r:r + MEM_ROWS, h * HEAD_DIM:(h + 1) * HEAD_DIM], mem_kv[:, h * HEAD_DIM:(h + 1) * HEAD_DIM])
         * scale for r, h in items]
    e = [jnp.exp(x - jnp.max(x, axis=-1, keepdims=True)).astype(BF16) for x in s]
    ones = jnp.ones((mem_kv.shape[0], HEAD_DIM), BF16)
    den = [_dot(x, ones) for x in e]
    o = [_dot(x, mem_kv[:, MEM_WIDTH + h * HEAD_DIM:MEM_WIDTH + (h + 1) * HEAD_DIM]) / d
         for x, d, (r, h) in zip(e, den, items)]
    chunks = [jnp.concatenate(o[i:i + N_MEM_HEADS], axis=1) for i in range(0, len(o), N_MEM_HEADS)]
    return jnp.concatenate(chunks, axis=0)


def _out_ln(mix_bf16, mq_ref, memkv_ref, x_ref, wout_ref, g_ref, b_ref, o_ref, *, alpha):
    c = mix_bf16.shape[1]
    mem_o = _memory_attention(mq_ref[...], memkv_ref[0])
    y = _dot(mix_bf16, wout_ref[:c, :]) + _dot(mem_o.astype(BF16), wout_ref[c:, :])
    o_ref[...] = _layer_norm(alpha * x_ref[...] + y, g_ref[...], b_ref[...])


def _sb_out_kernel(mix_ref, mq_ref, memkv_ref, x_ref, wout_ref, g_ref, b_ref, o_ref, *, alpha):
    _out_ln(mix_ref[...], mq_ref, memkv_ref, x_ref, wout_ref, g_ref, b_ref, o_ref, alpha=alpha)


def _rwkv_out_kernel(raw_ref, bonus_ref, gate_ref, lg_ref, lb_ref, sel_ref, back_ref,
                     mq_ref, memkv_ref, x_ref, wout_ref, g_ref, b_ref, o_ref, *, alpha):
    inv_n = 1.0 / HEAD_DIM
    mixes = []
    for r in range(0, raw_ref.shape[0], MEM_ROWS):
        rows = slice(r, r + MEM_ROWS)
        raw = raw_ref[rows, :]
        m = _head_sum(raw, sel_ref, back_ref, 3) * inv_n
        d = raw - m
        var = _head_sum(d * d, sel_ref, back_ref, 2) * inv_n
        normed = d * lax.rsqrt(var + LNX_EPS) * lg_ref[...] + lb_ref[...]
        mixes.append(((normed + bonus_ref[rows, :]) * gate_ref[rows, :]).astype(BF16))
    _out_ln(jnp.concatenate(mixes, axis=0), mq_ref, memkv_ref, x_ref, wout_ref, g_ref, b_ref, o_ref, alpha=alpha)


def _mix_out_ln(mix_inputs, mq, mq_col, mem_kv, x, w_out, g, b, *, alpha, seq, tm, rwkv):
    t, d = x.shape
    tiles_per_seq = seq // tm
    row = lambda i: (i, 0)
    const = lambda i: (0, 0)
    c = d - MEM_WIDTH
    if rwkv:
        kern = _rwkv_out_kernel
        mix_specs = ([pl.BlockSpec((tm, c), row)] * 3 + [pl.BlockSpec((1, c), const)] * 2
                     + [pl.BlockSpec(a.shape, const) for a in mix_inputs[5:]])
    else:
        kern = _sb_out_kernel
        mix_specs = [pl.BlockSpec((tm, c), row)]
    return pl.pallas_call(
        functools.partial(kern, alpha=alpha),
        grid=(t // tm,),
        in_specs=mix_specs + [
            pl.BlockSpec((tm, MEM_WIDTH), lambda i: (i, mq_col)),
            pl.BlockSpec((1,) + mem_kv.shape[1:], lambda i: (i // tiles_per_seq, 0, 0)),
            pl.BlockSpec((tm, d), row),
            pl.BlockSpec((d, d), const),
            pl.BlockSpec((1, d), const),
            pl.BlockSpec((1, d), const),
        ],
        out_specs=pl.BlockSpec((tm, d), row),
        out_shape=jax.ShapeDtypeStruct((t, d), F32),
        compiler_params=_cparams("parallel"),
        name="rwkv_out_ln" if rwkv else "sb_out_ln",
    )(*mix_inputs, mq, mem_kv, x, w_out, g, b)


def kernel(x, mem, ffn1_w_gate, ffn1_w_up, ffn1_w_down, ffn2_w_gate, ffn2_w_up, ffn2_w_down, ln_g, ln_b, w_out, w_mem_kv, rwkv_w_in, rwkv_mu, rwkv_w0, rwkv_w_up, rwkv_a0, rwkv_a_up, rwkv_g_up, rwkv_k_k, rwkv_k_a, rwkv_r_k, rwkv_lnx_g, rwkv_lnx_b, sb_w_in):
    bsz, seq, d = x.shape
    depth = ln_g.shape[0]
    c = d - MEM_WIDTH
    alpha = (2 * depth) ** 0.25
    n_mem = mem.shape[1]
    tm = min(512, seq)
    tf = 256

    head_of = jnp.arange(c) // HEAD_DIM
    sel = (head_of[:, None] == jnp.arange(LANES)[None, :]).astype(BF16)
    back = jnp.tile(sel.T, (HEAD_SUM_TERMS, 1))

    mem_kv = _proj(mem.reshape(bsz * n_mem, d), w_mem_kv.astype(BF16), jnp.ones((1, 2 * MEM_WIDTH), F32),
                   tm=256, out_dtype=BF16)
    mem_kv = mem_kv.reshape(bsz, n_mem, 2 * MEM_WIDTH)

    def ffn(xf, wg, wu, wd, g, b):
        return _ffn_ln(xf, wg.astype(BF16), wu.astype(BF16), wd.astype(BF16), g[None], b[None],
                       alpha=alpha, tm=tm, tf=tf)

    xf = x.reshape(bsz * seq, d)
    for i in range(depth):
        xf = ffn(xf, ffn1_w_gate[i], ffn1_w_up[i], ffn1_w_down[i], ln_g[i, 0], ln_b[i, 0])
        j = i // 2
        wo = w_out[i].astype(BF16)
        if i % 2 == 0:
            lora = jnp.zeros((LORA_WIDTH, 3 * c), F32)
            lora = lora.at[:DECAY_LORA, :c].set(rwkv_w_up[j])
            lora = lora.at[DECAY_LORA:DECAY_LORA + AAA_LORA, c:2 * c].set(rwkv_a_up[j])
            lora = lora.at[DECAY_LORA + AAA_LORA:, 2 * c:].set(rwkv_g_up[j])
            r, lw, k, v, kk, b, gate, bonus, mq = _rwkv_prep(
                xf, rwkv_w_in[j].astype(BF16), rwkv_mu[j][None], lora.astype(BF16),
                rwkv_w0[j][None], rwkv_a0[j][None], rwkv_k_k[j][None], rwkv_k_a[j][None],
                rwkv_r_k[j].reshape(1, c), sel, back, seq=seq, tm=256)
            sh = (bsz, seq, c)
            raw = _rwkv_chunk(r.reshape(sh), lw.reshape(sh), k.reshape(sh), v.reshape(sh),
                              kk.reshape(sh), b.reshape(sh), n_chunks=RWKV_CHUNKS_PER_STEP).reshape(bsz * seq, c)
            xf = _mix_out_ln((raw, bonus, gate, rwkv_lnx_g[j][None], rwkv_lnx_b[j][None], sel, back), mq, 0, mem_kv,
                             xf, wo, ln_g[i, 1][None], ln_b[i, 1][None], alpha=alpha, seq=seq, tm=tm, rwkv=True)
        else:
            q_scale = math.log2(math.e) / math.sqrt(HEAD_DIM)
            col_scale = jnp.where(jnp.arange(3 * c + MEM_WIDTH) < c, q_scale, 1.0).astype(F32)[None]
            p, ksq = _sb_proj(xf, sb_w_in[j].astype(BF16), col_scale, c=c, tm=tm)
            ksq = jnp.max(ksq.reshape(bsz, -1), axis=1)
            ksq = jnp.broadcast_to(ksq[:, None, None], (bsz, SUBLANES, LANES))
            mix = _sb_attention(p, ksq, bsz=bsz, seq=seq, c=c)
            xf = _mix_out_ln((mix,), p, (3 * c) // MEM_WIDTH, mem_kv, xf, wo, ln_g[i, 1][None], ln_b[i, 1][None],
                             alpha=alpha, seq=seq, tm=tm, rwkv=False)
        xf = ffn(xf, ffn2_w_gate[i], ffn2_w_up[i], ffn2_w_down[i], ln_g[i, 2], ln_b[i, 2])
    return xf.reshape(bsz, seq, d)
```

```python
import functools
import math

import jax
import jax.numpy as jnp
from jax import lax
from jax.experimental import pallas as pl
from jax.experimental.pallas import tpu as pltpu

F32 = jnp.float32
BF16 = jnp.bfloat16

HEAD_DIM = 64
LANES = 128
SUBLANES = 8
HEAD_SUM_TERMS = 3
N_MEM_HEADS = 4
MEM_WIDTH = N_MEM_HEADS * HEAD_DIM
DECAY_LORA = 64
AAA_LORA = 64
GATE_LORA = 128
LORA_WIDTH = DECAY_LORA + AAA_LORA + GATE_LORA
LN_EPS = 1e-5
LNX_EPS = 64e-5
RWKV_CHUNK = 64
RWKV_CHUNKS_PER_STEP = 4
INV_BLOCK = 16
MEM_ROWS = 128
SB_BLOCK = 256
SB_GROUP = 6
SB_HEADS = 12
SB_TERMS = 2
SB_DEAD_LOG2 = 192.0
SB_BOUND_MARGIN = 1.05
VMEM_LIMIT = 48 * 1024 * 1024


def _cparams(*sem):
    return pltpu.CompilerParams(dimension_semantics=sem, vmem_limit_bytes=VMEM_LIMIT)


def _dot(a, b):
    return jnp.dot(a, b, preferred_element_type=F32)


def _dot_nt(a, b):
    return lax.dot_general(a, b, (((1,), (1,)), ((), ())), preferred_element_type=F32)


def _dot_tn(a, b):
    return lax.dot_general(a, b, (((0,), (0,)), ((), ())), preferred_element_type=F32)


def _split_dot(x, ones, terms):
    acc = None
    rem = x
    for _ in range(terms):
        piece = rem.astype(BF16)
        part = _dot(piece, ones)
        acc = part if acc is None else acc + part
        rem = rem - piece.astype(F32)
    return acc


def _head_sum(x, sel_ref, back_ref, terms):
    s = _split_dot(x, sel_ref[...], terms)
    pieces = []
    for _ in range(terms):
        piece = s.astype(BF16)
        pieces.append(piece)
        s = s - piece.astype(F32)
    return _dot(jnp.concatenate(pieces, axis=1), back_ref[:terms * LANES, :])


def _softplus(x):
    return jnp.maximum(x, 0.0) + jnp.log(1.0 + jnp.exp(-jnp.abs(x)))


def _layer_norm(y, g, b):
    mu = jnp.mean(y, axis=-1, keepdims=True)
    d = y - mu
    var = jnp.mean(d * d, axis=-1, keepdims=True)
    return d * lax.rsqrt(var + LN_EPS) * g + b


def _ffn_kernel(x_ref, wg_ref, wu_ref, wd_ref, g_ref, b_ref, o_ref, h_ref, *, alpha, tf):
    x = x_ref[...]
    xb = x.astype(BF16)
    for j in range(wg_ref.shape[1] // tf):
        cols = slice(j * tf, (j + 1) * tf)
        gate = _dot(xb, wg_ref[:, cols])
        up = _dot(xb, wu_ref[:, cols])
        h_ref[:, cols] = (gate * jax.nn.sigmoid(gate) * up).astype(BF16)
    y = alpha * x + 0.5 * _dot(h_ref[...], wd_ref[...])
    o_ref[...] = _layer_norm(y, g_ref[...], b_ref[...])


def _resident(shape):
    return pl.BlockSpec(shape, lambda i: (0,) * len(shape), pipeline_mode=pl.Buffered(1))


def _resident_layer(shape, layer):
    return pl.BlockSpec((None,) + shape, lambda i: (layer,) + (0,) * len(shape), pipeline_mode=pl.Buffered(1))


def _ffn_ln(x, wg, wu, wd, g, b, layer, *, alpha, tm, tf):
    t, d = x.shape
    f = wg.shape[2]
    return pl.pallas_call(
        functools.partial(_ffn_kernel, alpha=alpha, tf=tf),
        grid=(t // tm,),
        in_specs=[
            pl.BlockSpec((tm, d), lambda i: (i, 0)),
            _resident_layer((d, f), layer), _resident_layer((d, f), layer), _resident_layer((f, d), layer),
            _resident((1, d)), _resident((1, d)),
        ],
        out_specs=pl.BlockSpec((tm, d), lambda i: (i, 0)),
        out_shape=jax.ShapeDtypeStruct((t, d), F32),
        scratch_shapes=[pltpu.VMEM((tm, f), BF16)],
        compiler_params=_cparams("parallel"),
        name="ffn_ln",
    )(x, wg, wu, wd, g, b)


def _proj_kernel(x_ref, w_ref, s_ref, o_ref):
    o_ref[...] = (_dot(x_ref[...].astype(BF16), w_ref[...]) * s_ref[...]).astype(o_ref.dtype)


def _proj(x, w, col_scale, *, tm, out_dtype):
    t, d = x.shape
    n = w.shape[1]
    return pl.pallas_call(
        _proj_kernel,
        grid=(t // tm,),
        in_specs=[pl.BlockSpec((tm, d), lambda i: (i, 0)), _resident((d, n)), _resident((1, n))],
        out_specs=pl.BlockSpec((tm, n), lambda i: (i, 0)),
        out_shape=jax.ShapeDtypeStruct((t, n), out_dtype),
        compiler_params=_cparams("parallel"),
        name="proj",
    )(x, w, col_scale)


def _rwkv_prep_kernel(x_ref, win_ref, mu_ref, lora_ref, w0_ref, a0_ref, kkw_ref, kaw_ref, rkw_ref, sel_ref, back_ref,
                      r_o, lw_o, k_o, v_o, kk_o, b_o, g_o, bonus_o, mq_o, carry_ref, *, tiles_per_seq, c):
    i = pl.program_id(0)
    shift = 3 * c + LORA_WIDTH

    @pl.when(i % tiles_per_seq == 0)
    def _():
        carry_ref[...] = jnp.zeros_like(carry_ref)

    p = _dot(x_ref[...].astype(BF16), win_ref[...])
    tm = p.shape[0]
    ps = p[:, :shift]
    mq_o[...] = p[:, shift:].astype(mq_o.dtype)

    row = lax.broadcasted_iota(jnp.int32, ps.shape, 0)
    prev = jnp.where(row == 0, carry_ref[...], pltpu.roll(ps, 1, 0))
    carry_ref[...] = ps[tm - 1:tm, :]
    z = ps + (prev - ps) * mu_ref[...]

    r = z[:, :c]
    kraw = z[:, c:2 * c]
    v = z[:, 2 * c:3 * c]
    lo = z[:, 3 * c:]
    lane = lax.broadcasted_iota(jnp.int32, lo.shape, 1)
    lhs = jnp.where(lane < DECAY_LORA, jnp.tanh(lo),
                    jnp.where(lane < DECAY_LORA + AAA_LORA, lo, jax.nn.sigmoid(lo)))
    pre = _dot(lhs.astype(BF16), lora_ref[...])
    w = -_softplus(-(pre[:, :c] + w0_ref[...])) - 0.5
    a = jax.nn.sigmoid(pre[:, c:2 * c] + a0_ref[...])
    g_o[...] = pre[:, 2 * c:]

    kkv = kraw * kkw_ref[...]
    ss = _head_sum(kkv * kkv, sel_ref, back_ref, 2)
    kkn = kkv * lax.rsqrt(jnp.maximum(ss, 1e-24))
    k2 = kraw * (1.0 + (a - 1.0) * kaw_ref[...])

    r_o[...] = r
    lw_o[...] = -jnp.exp(w)
    k_o[...] = k2
    v_o[...] = v
    kk_o[...] = kkn
    b_o[...] = kkn * a
    bonus_o[...] = _head_sum(r * k2 * rkw_ref[...], sel_ref, back_ref, 2) * v


def _rwkv_prep(x, w_in, mu, lora, w0, a0, k_k, k_a, r_k, sel, back, *, seq, tm):
    t, d = x.shape
    c = w0.shape[1]
    n_in = w_in.shape[1]
    shift = 3 * c + LORA_WIDTH
    row = lambda i: (i, 0)
    const = lambda i: (0, 0)
    wide = jax.ShapeDtypeStruct((t, c), F32)
    return pl.pallas_call(
        functools.partial(_rwkv_prep_kernel, tiles_per_seq=seq // tm, c=c),
        grid=(t // tm,),
        in_specs=[
            pl.BlockSpec((tm, d), row),
            pl.BlockSpec((d, n_in), const),
            pl.BlockSpec((1, shift), const),
            pl.BlockSpec((LORA_WIDTH, 3 * c), const),
            pl.BlockSpec((1, c), const), pl.BlockSpec((1, c), const), pl.BlockSpec((1, c), const),
            pl.BlockSpec((1, c), const), pl.BlockSpec((1, c), const),
            pl.BlockSpec(sel.shape, const), pl.BlockSpec(back.shape, const),
        ],
        out_specs=[pl.BlockSpec((tm, c), row)] * 8 + [pl.BlockSpec((tm, MEM_WIDTH), row)],
        out_shape=[wide] * 8 + [jax.ShapeDtypeStruct((t, MEM_WIDTH), BF16)],
        scratch_shapes=[pltpu.VMEM((1, shift), F32)],
        compiler_params=_cparams("arbitrary"),
        name="rwkv_prep",
    )(x, w_in, mu, lora, w0, a0, k_k, k_a, r_k, sel, back)


def _rwkv_chunk_kernel(r_ref, lw_ref, k_ref, v_ref, kk_ref, b_ref, o_ref, s_ref, *, n_pairs, n_chunks):
    @pl.when(pl.program_id(1) == 0)
    def _():
        s_ref[...] = jnp.zeros_like(s_ref)

    L = RWKV_CHUNK
    width = 2 * HEAD_DIM
    ti = lax.broadcasted_iota(jnp.int32, (L, width), 0)
    lane = lax.broadcasted_iota(jnp.int32, (L, width), 1)
    si = lane % HEAD_DIM
    first = lane < HEAD_DIM
    incl = si <= ti
    strict = si < ti
    eye = (si == ti).astype(F32)
    same_block = (si // INV_BLOCK) == (ti // INV_BLOCK)
    tri = lax.broadcasted_iota(jnp.int32, (L, L), 0) >= lax.broadcasted_iota(jnp.int32, (L, L), 1)
    tril_ones = tri.astype(BF16)

    def bdiag(x):
        x = x.astype(BF16)
        zero = jnp.zeros_like(x)
        return jnp.concatenate([jnp.where(first, x, zero), jnp.where(first, zero, x)], axis=0)

    def pdot(a, b):
        return _dot(a.astype(BF16), bdiag(b))

    items = []
    for ci in range(n_chunks):
        rows = slice(ci * L, (ci + 1) * L)
        lw = lw_ref[0, rows, :]
        cum = _split_dot_left(tril_ones, lw, 3)
        cum_last = cum[L - 1:L, :]
        g_inv = jnp.exp(-cum)
        g_rel = jnp.exp(cum_last - cum)
        kk = kk_ref[0, rows, :]
        b = b_ref[0, rows, :]
        k = k_ref[0, rows, :]
        kap = kk * jnp.exp(cum - lw)
        rho = r_ref[0, rows, :] * jnp.exp(cum)
        bet = b * g_inv
        kt = k * g_inv
        bg = b * g_rel
        kg = k * g_rel
        v = v_ref[0, rows, :]
        g_last = jnp.exp(cum_last)
        for p in range(n_pairs):
            sl = slice(p * width, (p + 1) * width)
            items.append(dict(kap=kap[:, sl], rho=rho[:, sl], bet=bet[:, sl], kt=kt[:, sl], bg=bg[:, sl],
                              kg=kg[:, sl], v=v[:, sl], g_last=g_last[:, sl]))
    prod = [_dot_nt(jnp.concatenate([it["kap"], it["rho"]], axis=0).astype(BF16),
                    jnp.concatenate([bdiag(it["bet"]), bdiag(it["kt"])], axis=0)) for it in items]
    m_ab = [jnp.where(strict, x[:L, :width], 0.0) for x in prod]
    m_ak = [jnp.where(strict, x[:L, width:], 0.0) for x in prod]
    a_r = [jnp.concatenate([jnp.where(incl, x[L:, :width], 0.0), jnp.where(incl, x[L:, width:], 0.0)],
                           axis=1).astype(BF16) for x in prod]

    md = [jnp.where(same_block, m, 0.0) for m in m_ab]
    nd = [m - d for m, d in zip(m_ab, md)]
    m2 = [pdot(d, d) for d in md]
    m4 = [pdot(x, x) for x in m2]
    m8 = [pdot(x, x) for x in m4]
    di = [eye - d for d in md]
    di = [x + pdot(x, q) for x, q in zip(di, m2)]
    di = [x + pdot(x, q) for x, q in zip(di, m4)]
    di = [x + pdot(x, q) for x, q in zip(di, m8)]
    xn = [pdot(x, y) for x, y in zip(di, nd)]
    x2 = [pdot(x, x) for x in xn]
    yi = [eye - x for x in xn]
    yi = [a + pdot(a, q) for a, q in zip(yi, x2)]
    t_inv = [pdot(a, x) for a, x in zip(yi, di)]

    bd_v = [bdiag(it["v"]) for it in items]
    mv = [_dot(m.astype(BF16), x) for m, x in zip(m_ak, bd_v)]
    wu = [_dot(t.astype(BF16), jnp.concatenate([bdiag(it["kap"]), bdiag(x)], axis=1))
          for t, it, x in zip(t_inv, items, mv)]

    state = [s_ref[p] for p in range(n_pairs)]
    for ci in range(n_chunks):
        idx = range(ci * n_pairs, (ci + 1) * n_pairs)
        ws = [_dot_nt(jnp.concatenate([wu[i][:, :width], items[i]["rho"]], axis=0).astype(BF16), bdiag(s))
              for i, s in zip(idx, state)]
        u = [-wu[i][:, width:] - x[:L] for i, x in zip(idx, ws)]
        for n, (i, x, y) in enumerate(zip(idx, ws, u)):
            o_ref[0, ci * L:(ci + 1) * L, n * width:(n + 1) * width] = (
                x[L:] + _dot(a_r[i], jnp.concatenate([bdiag(y), bd_v[i]], axis=0)))
        upd = [_dot_tn(jnp.concatenate([y, items[i]["v"]], axis=0).astype(BF16),
                       jnp.concatenate([items[i]["bg"], items[i]["kg"]], axis=0).astype(BF16))
               for i, y in zip(idx, u)]
        state = [s * items[i]["g_last"] + jnp.where(first, x[:L], x[L:]) for i, s, x in zip(idx, state, upd)]
    for p in range(n_pairs):
        s_ref[p] = state[p]


def _split_dot_left(ones, x, terms):
    acc = None
    rem = x
    for _ in range(terms):
        piece = rem.astype(BF16)
        part = _dot(ones, piece)
        acc = part if acc is None else acc + part
        rem = rem - piece.astype(F32)
    return acc


def _rwkv_chunk(r, lw, k, v, kk, b, *, n_chunks):
    bsz, seq, c = r.shape
    n_pairs = c // (2 * HEAD_DIM)
    rows = n_chunks * RWKV_CHUNK
    blk = pl.BlockSpec((1, rows, c), lambda bi, ci: (bi, ci, 0))
    return pl.pallas_call(
        functools.partial(_rwkv_chunk_kernel, n_pairs=n_pairs, n_chunks=n_chunks),
        grid=(bsz, seq // rows),
        in_specs=[blk] * 6,
        out_specs=blk,
        out_shape=jax.ShapeDtypeStruct((bsz, seq, c), F32),
        scratch_shapes=[pltpu.VMEM((n_pairs, HEAD_DIM, 2 * HEAD_DIM), F32)],
        compiler_params=_cparams("parallel", "arbitrary"),
        name="rwkv_chunk",
    )(r, lw, k, v, kk, b)


def _max_sq_norm(x):
    width = x.shape[1]
    chan = lax.broadcasted_iota(jnp.int32, (width, LANES), 0) // HEAD_DIM
    sel = (chan == lax.broadcasted_iota(jnp.int32, (width, LANES), 1)).astype(BF16)
    xf = x.astype(F32)
    return jnp.max(_dot((xf * xf).astype(BF16), sel), keepdims=True)


def _sb_proj_kernel(x_ref, w_ref, s_ref, o_ref, ksq_ref, *, c):
    p = (_dot(x_ref[...].astype(BF16), w_ref[...]) * s_ref[...]).astype(o_ref.dtype)
    o_ref[...] = p
    ksq_ref[...] = jnp.broadcast_to(_max_sq_norm(p[:, c:2 * c]), ksq_ref.shape)


def _sb_proj(x, w, col_scale, *, c, tm):
    t, d = x.shape
    n = w.shape[1]
    return pl.pallas_call(
        functools.partial(_sb_proj_kernel, c=c),
        grid=(t // tm,),
        in_specs=[pl.BlockSpec((tm, d), lambda i: (i, 0)), _resident((d, n)), _resident((1, n))],
        out_specs=[pl.BlockSpec((tm, n), lambda i: (i, 0)), pl.BlockSpec((SUBLANES, LANES), lambda i: (i, 0))],
        out_shape=[jax.ShapeDtypeStruct((t, n), BF16),
                   jax.ShapeDtypeStruct((t // tm * SUBLANES, LANES), F32)],
        compiler_params=_cparams("parallel"),
        name="sb_proj",
    )(x, w, col_scale)


def _sb_kernel(q_ref, k_ref, v_ref, ksq_ref, o_ref, *, n_heads):
    qi = pl.program_id(2)
    tq = q_ref.shape[0]
    ti = lax.broadcasted_iota(jnp.int32, (tq, tq), 0)
    si = lax.broadcasted_iota(jnp.int32, (tq, tq), 1)
    before = si < ti
    suffix = jnp.concatenate([(ti >= si).astype(BF16)] * SB_TERMS, axis=0)
    q_all = q_ref[...]
    n_groups = n_heads // SB_GROUP

    def key_block(j, carries, accs, diagonal):
        rows = pl.ds(pl.multiple_of(j * tq, tq), tq)
        kb = k_ref[rows, :]
        vb = v_ref[rows, :]
        ys, sums = {}, {}
        new_carries, new_accs = list(carries), list(accs)

        def scores(g):
            sps = []
            for h in range(g * SB_GROUP, (g + 1) * SB_GROUP):
                sl = slice(h * HEAD_DIM, (h + 1) * HEAD_DIM)
                y = _dot_nt(q_all[:, sl], kb[:, sl])
                sp = jnp.maximum(y, 0.0) + jnp.log2(1.0 + jnp.exp2(-jnp.abs(y)))
                if diagonal:
                    sp = jnp.where(before, sp, 0.0)
                ys[h] = y
                pieces = []
                for _ in range(SB_TERMS):
                    piece = sp.astype(BF16)
                    pieces.append(piece)
                    sp = sp - piece.astype(F32)
                sps.append(jnp.concatenate(pieces, axis=1))
            out = _dot(jnp.concatenate(sps, axis=0), suffix)
            for i, h in enumerate(range(g * SB_GROUP, (g + 1) * SB_GROUP)):
                sums[h] = out[i * tq:(i + 1) * tq, :]

        def weights(g):
            for h in range(g * SB_GROUP, (g + 1) * SB_GROUP):
                sl = slice(h * HEAD_DIM, (h + 1) * HEAD_DIM)
                att = jnp.exp2(ys[h] - sums[h] - carries[h])
                if diagonal:
                    att = jnp.where(before, att, 0.0)
                new_accs[h] = accs[h] + _dot(att.astype(BF16), vb[:, sl])
                new_carries[h] = carries[h] + sums[h][:, 0:1]

        for g in range(n_groups + 1):
            if g < n_groups:
                scores(g)
            if g > 0:
                weights(g - 1)
        return tuple(new_carries), tuple(new_accs)

    zeros_c = tuple(jnp.zeros((tq, 1), F32) for _ in range(n_heads))
    zeros_a = tuple(jnp.zeros((tq, HEAD_DIM), F32) for _ in range(n_heads))
    carries, accs = key_block(qi, zeros_c, zeros_a, True)

    bound = jnp.sqrt(_max_sq_norm(q_all) * ksq_ref[0, 0:1, 0:1]) * SB_BOUND_MARGIN

    def alive(state):
        jj, carries, _ = state
        slack = functools.reduce(jnp.minimum, [jnp.min(c, keepdims=True) for c in carries]) - bound
        return jnp.logical_and(jj < qi, slack[0, 0] < SB_DEAD_LOG2)

    def body(state):
        jj, carries, accs = state
        carries, accs = key_block(qi - 1 - jj, carries, accs, False)
        return jj + 1, carries, accs

    _, _, accs = lax.while_loop(alive, body, (jnp.int32(0), carries, accs))
    o_ref[...] = jnp.concatenate(accs, axis=1).astype(o_ref.dtype)


def _sb_attention(p, ksq, *, bsz, seq, c):
    pair = SB_HEADS * HEAD_DIM
    n_pairs = c // pair
    nq = seq // SB_BLOCK
    return pl.pallas_call(
        functools.partial(_sb_kernel, n_heads=SB_HEADS),
        grid=(bsz, n_pairs, nq),
        in_specs=[
            pl.BlockSpec((SB_BLOCK, pair), lambda b, h, i: (b * nq + i, h)),
            pl.BlockSpec((seq, pair), lambda b, h, i: (b, n_pairs + h)),
            pl.BlockSpec((seq, pair), lambda b, h, i: (b, 2 * n_pairs + h)),
            pl.BlockSpec((1, SUBLANES, LANES), lambda b, h, i: (b, 0, 0)),
        ],
        out_specs=pl.BlockSpec((SB_BLOCK, pair), lambda b, h, i: (b * nq + i, h)),
        out_shape=jax.ShapeDtypeStruct((bsz * seq, c), BF16),
        compiler_params=_cparams("parallel", "parallel", "arbitrary"),
        name="sb_attention",
    )(p, p, p, ksq)


def _memory_attention(mq, mem_kv):
    scale = 1.0 / math.sqrt(HEAD_DIM)
    tm = mq.shape[0]
    items = [(r, h) for r in range(0, tm, MEM_ROWS) for h in range(N_MEM_HEADS)]
    s = [_dot_nt(mq[r:r + MEM_ROWS, h * HEAD_DIM:(h + 1) * HEAD_DIM], mem_kv[:, h * HEAD_DIM:(h + 1) * HEAD_DIM])
         * scale for r, h in items]
    e = [jnp.exp(x - jnp.max(x, axis=-1, keepdims=True)).astype(BF16) for x in s]
    ones = jnp.ones((mem_kv.shape[0], HEAD_DIM), BF16)
    den = [_dot(x, ones) for x in e]
    o = [_dot(x, mem_kv[:, MEM_WIDTH + h * HEAD_DIM:MEM_WIDTH + (h + 1) * HEAD_DIM]) / d
         for x, d, (r, h) in zip(e, den, items)]
    chunks = [jnp.concatenate(o[i:i + N_MEM_HEADS], axis=1) for i in range(0, len(o), N_MEM_HEADS)]
    return jnp.concatenate(chunks, axis=0)


def _out_ln(mix_bf16, mq_ref, memkv_ref, x_ref, wout_ref, g_ref, b_ref, o_ref, *, alpha):
    c = mix_bf16.shape[1]
    mem_o = _memory_attention(mq_ref[...], memkv_ref[0])
    y = _dot(mix_bf16, wout_ref[:c, :]) + _dot(mem_o.astype(BF16), wout_ref[c:, :])
    o_ref[...] = _layer_norm(alpha * x_ref[...] + y, g_ref[...], b_ref[...])


def _sb_out_kernel(mix_ref, mq_ref, memkv_ref, x_ref, wout_ref, g_ref, b_ref, o_ref, *, alpha):
    _out_ln(mix_ref[...], mq_ref, memkv_ref, x_ref, wout_ref, g_ref, b_ref, o_ref, alpha=alpha)


def _rwkv_out_kernel(raw_ref, bonus_ref, gate_ref, lg_ref, lb_ref, sel_ref, back_ref,
                     mq_ref, memkv_ref, x_ref, wout_ref, g_ref, b_ref, o_ref, *, alpha):
    inv_n = 1.0 / HEAD_DIM
    mixes = []
    for r in range(0, raw_ref.shape[0], MEM_ROWS):
        rows = slice(r, r + MEM_ROWS)
        raw = raw_ref[rows, :]
        m = _head_sum(raw, sel_ref, back_ref, 3) * inv_n
        d = raw - m
        var = _head_sum(d * d, sel_ref, back_ref, 2) * inv_n
        normed = d * lax.rsqrt(var + LNX_EPS) * lg_ref[...] + lb_ref[...]
        mixes.append(((normed + bonus_ref[rows, :]) * gate_ref[rows, :]).astype(BF16))
    _out_ln(jnp.concatenate(mixes, axis=0), mq_ref, memkv_ref, x_ref, wout_ref, g_ref, b_ref, o_ref, alpha=alpha)


def _mix_out_ln(mix_inputs, mq, mq_col, mem_kv, x, w_out, g, b, *, alpha, seq, tm, rwkv):
    t, d = x.shape
    tiles_per_seq = seq // tm
    row = lambda i: (i, 0)
    const = lambda i: (0, 0)
    c = d - MEM_WIDTH
    if rwkv:
        kern = _rwkv_out_kernel
        mix_specs = ([pl.BlockSpec((tm, c), row)] * 3 + [pl.BlockSpec((1, c), const)] * 2
                     + [pl.BlockSpec(a.shape, const) for a in mix_inputs[5:]])
    else:
        kern = _sb_out_kernel
        mix_specs = [pl.BlockSpec((tm, c), row)]
    return pl.pallas_call(
        functools.partial(kern, alpha=alpha),
        grid=(t // tm,),
        in_specs=mix_specs + [
            pl.BlockSpec((tm, MEM_WIDTH), lambda i: (i, mq_col)),
            pl.BlockSpec((1,) + mem_kv.shape[1:], lambda i: (i // tiles_per_seq, 0, 0)),
            pl.BlockSpec((tm, d), row),
            pl.BlockSpec((d, d), const),
            pl.BlockSpec((1, d), const),
            pl.BlockSpec((1, d), const),
        ],
        out_specs=pl.BlockSpec((tm, d), row),
        out_shape=jax.ShapeDtypeStruct((t, d), F32),
        compiler_params=_cparams("parallel"),
        name="rwkv_out_ln" if rwkv else "sb_out_ln",
    )(*mix_inputs, mq, mem_kv, x, w_out, g, b)


def kernel(x, mem, ffn1_w_gate, ffn1_w_up, ffn1_w_down, ffn2_w_gate, ffn2_w_up, ffn2_w_down, ln_g, ln_b, w_out, w_mem_kv, rwkv_w_in, rwkv_mu, rwkv_w0, rwkv_w_up, rwkv_a0, rwkv_a_up, rwkv_g_up, rwkv_k_k, rwkv_k_a, rwkv_r_k, rwkv_lnx_g, rwkv_lnx_b, sb_w_in):
    bsz, seq, d = x.shape
    depth = ln_g.shape[0]
    c = d - MEM_WIDTH
    alpha = (2 * depth) ** 0.25
    n_mem = mem.shape[1]
    tm = min(512, seq)
    tf = 256

    head_of = jnp.arange(c) // HEAD_DIM
    sel = (head_of[:, None] == jnp.arange(LANES)[None, :]).astype(BF16)
    back = jnp.tile(sel.T, (HEAD_SUM_TERMS, 1))

    mem_kv = _proj(mem.reshape(bsz * n_mem, d), w_mem_kv.astype(BF16), jnp.ones((1, 2 * MEM_WIDTH), F32),
                   tm=256, out_dtype=BF16)
    mem_kv = mem_kv.reshape(bsz, n_mem, 2 * MEM_WIDTH)

    ffn1 = tuple(w.astype(BF16) for w in (ffn1_w_gate, ffn1_w_up, ffn1_w_down))
    ffn2 = tuple(w.astype(BF16) for w in (ffn2_w_gate, ffn2_w_up, ffn2_w_down))

    def ffn(xf, weights, layer, g, b):
        return _ffn_ln(xf, *weights, g[None], b[None], layer, alpha=alpha, tm=tm, tf=tf)

    xf = x.reshape(bsz * seq, d)
    for i in range(depth):
        xf = ffn(xf, ffn1, i, ln_g[i, 0], ln_b[i, 0])
        j = i // 2
        wo = w_out[i].astype(BF16)
        if i % 2 == 0:
            lora = jnp.zeros((LORA_WIDTH, 3 * c), F32)
            lora = lora.at[:DECAY_LORA, :c].set(rwkv_w_up[j])
            lora = lora.at[DECAY_LORA:DECAY_LORA + AAA_LORA, c:2 * c].set(rwkv_a_up[j])
            lora = lora.at[DECAY_LORA + AAA_LORA:, 2 * c:].set(rwkv_g_up[j])
            r, lw, k, v, kk, b, gate, bonus, mq = _rwkv_prep(
                xf, rwkv_w_in[j].astype(BF16), rwkv_mu[j][None], lora.astype(BF16),
                rwkv_w0[j][None], rwkv_a0[j][None], rwkv_k_k[j][None], rwkv_k_a[j][None],
                rwkv_r_k[j].reshape(1, c), sel, back, seq=seq, tm=256)
            sh = (bsz, seq, c)
            raw = _rwkv_chunk(r.reshape(sh), lw.reshape(sh), k.reshape(sh), v.reshape(sh),
                              kk.reshape(sh), b.reshape(sh), n_chunks=RWKV_CHUNKS_PER_STEP).reshape(bsz * seq, c)
            xf = _mix_out_ln((raw, bonus, gate, rwkv_lnx_g[j][None], rwkv_lnx_b[j][None], sel, back), mq, 0, mem_kv,
                             xf, wo, ln_g[i, 1][None], ln_b[i, 1][None], alpha=alpha, seq=seq, tm=tm, rwkv=True)
        else:
            q_scale = math.log2(math.e) / math.sqrt(HEAD_DIM)
            col_scale = jnp.where(jnp.arange(3 * c + MEM_WIDTH) < c, q_scale, 1.0).astype(F32)[None]
            p, ksq = _sb_proj(xf, sb_w_in[j].astype(BF16), col_scale, c=c, tm=tm)
            ksq = jnp.max(ksq.reshape(bsz, -1), axis=1)
            ksq = jnp.broadcast_to(ksq[:, None, None], (bsz, SUBLANES, LANES))
            mix = _sb_attention(p, ksq, bsz=bsz, seq=seq, c=c)
            xf = _mix_out_ln((mix,), p, (3 * c) // MEM_WIDTH, mem_kv, xf, wo, ln_g[i, 1][None], ln_b[i, 1][None],
                             alpha=alpha, seq=seq, tm=tm, rwkv=False)
        xf = ffn(xf, ffn2, i, ln_g[i, 2], ln_b[i, 2])
    return xf.reshape(bsz, seq, d)
```

```python
import functools
import math

import jax
import jax.numpy as jnp
from jax import lax
from jax.experimental import pallas as pl
from jax.experimental.pallas import tpu as pltpu

F32 = jnp.float32
BF16 = jnp.bfloat16

HEAD_DIM = 64
LANES = 128
SUBLANES = 8
HEAD_SUM_TERMS = 3
N_MEM_HEADS = 4
MEM_WIDTH = N_MEM_HEADS * HEAD_DIM
DECAY_LORA = 64
AAA_LORA = 64
GATE_LORA = 128
LORA_WIDTH = DECAY_LORA + AAA_LORA + GATE_LORA
LN_EPS = 1e-5
LNX_EPS = 64e-5
RWKV_CHUNK = 64
RWKV_CHUNKS_PER_STEP = 4
INV_BLOCK = 16
MEM_ROWS = 256
SB_BLOCK = 256
SB_GROUP = 6
SB_HEADS = 12
SB_TERMS = 2
SB_DEAD_LOG2 = 192.0
SB_BOUND_MARGIN = 1.05
VMEM_LIMIT = 48 * 1024 * 1024


def _cparams(*sem):
    return pltpu.CompilerParams(dimension_semantics=sem, vmem_limit_bytes=VMEM_LIMIT)


def _dot(a, b):
    return jnp.dot(a, b, preferred_element_type=F32)


def _dot_nt(a, b):
    return lax.dot_general(a, b, (((1,), (1,)), ((), ())), preferred_element_type=F32)


def _dot_tn(a, b):
    return lax.dot_general(a, b, (((0,), (0,)), ((), ())), preferred_element_type=F32)


def _split_dot(x, ones, terms):
    acc = None
    rem = x
    for _ in range(terms):
        piece = rem.astype(BF16)
        part = _dot(piece, ones)
        acc = part if acc is None else acc + part
        rem = rem - piece.astype(F32)
    return acc


def _head_sum(x, sel_ref, back_ref, terms):
    s = _split_dot(x, sel_ref[...], terms)
    pieces = []
    for _ in range(terms):
        piece = s.astype(BF16)
        pieces.append(piece)
        s = s - piece.astype(F32)
    return _dot(jnp.concatenate(pieces, axis=1), back_ref[:terms * LANES, :])


def _softplus(x):
    return jnp.maximum(x, 0.0) + jnp.log(1.0 + jnp.exp(-jnp.abs(x)))


def _layer_norm(y, g, b):
    mu = jnp.mean(y, axis=-1, keepdims=True)
    d = y - mu
    var = jnp.mean(d * d, axis=-1, keepdims=True)
    return d * lax.rsqrt(var + LN_EPS) * g + b


def _ffn_kernel(x_ref, wg_ref, wu_ref, wd_ref, g_ref, b_ref, o_ref, h_ref, *, alpha, tf):
    x = x_ref[...]
    xb = x.astype(BF16)
    for j in range(wg_ref.shape[1] // tf):
        cols = slice(j * tf, (j + 1) * tf)
        gate = _dot(xb, wg_ref[:, cols])
        up = _dot(xb, wu_ref[:, cols])
        h_ref[:, cols] = (gate * jax.nn.sigmoid(gate) * up).astype(BF16)
    y = alpha * x + 0.5 * _dot(h_ref[...], wd_ref[...])
    o_ref[...] = _layer_norm(y, g_ref[...], b_ref[...])


def _resident(shape):
    return pl.BlockSpec(shape, lambda i: (0,) * len(shape), pipeline_mode=pl.Buffered(1))


def _resident_layer(shape, layer):
    return pl.BlockSpec((None,) + shape, lambda i: (layer,) + (0,) * len(shape), pipeline_mode=pl.Buffered(1))


def _ffn_ln(x, wg, wu, wd, g, b, layer, *, alpha, tm, tf):
    t, d = x.shape
    f = wg.shape[2]
    return pl.pallas_call(
        functools.partial(_ffn_kernel, alpha=alpha, tf=tf),
        grid=(t // tm,),
        in_specs=[
            pl.BlockSpec((tm, d), lambda i: (i, 0)),
            _resident_layer((d, f), layer), _resident_layer((d, f), layer), _resident_layer((f, d), layer),
            _resident((1, d)), _resident((1, d)),
        ],
        out_specs=pl.BlockSpec((tm, d), lambda i: (i, 0)),
        out_shape=jax.ShapeDtypeStruct((t, d), F32),
        scratch_shapes=[pltpu.VMEM((tm, f), BF16)],
        compiler_params=_cparams("parallel"),
        name="ffn_ln",
    )(x, wg, wu, wd, g, b)


def _proj_kernel(x_ref, w_ref, s_ref, o_ref):
    o_ref[...] = (_dot(x_ref[...].astype(BF16), w_ref[...]) * s_ref[...]).astype(o_ref.dtype)


def _proj(x, w, col_scale, *, tm, out_dtype):
    t, d = x.shape
    n = w.shape[1]
    return pl.pallas_call(
        _proj_kernel,
        grid=(t // tm,),
        in_specs=[pl.BlockSpec((tm, d), lambda i: (i, 0)), _resident((d, n)), _resident((1, n))],
        out_specs=pl.BlockSpec((tm, n), lambda i: (i, 0)),
        out_shape=jax.ShapeDtypeStruct((t, n), out_dtype),
        compiler_params=_cparams("parallel"),
        name="proj",
    )(x, w, col_scale)


def _rwkv_prep_kernel(x_ref, win_ref, mu_ref, lora_ref, w0_ref, a0_ref, kkw_ref, kaw_ref, rkw_ref, sel_ref, back_ref,
                      r_o, lw_o, k_o, v_o, kk_o, b_o, g_o, bonus_o, mq_o, carry_ref, *, tiles_per_seq, c):
    i = pl.program_id(0)
    shift = 3 * c + LORA_WIDTH

    @pl.when(i % tiles_per_seq == 0)
    def _():
        carry_ref[...] = jnp.zeros_like(carry_ref)

    p = _dot(x_ref[...].astype(BF16), win_ref[...])
    tm = p.shape[0]
    ps = p[:, :shift]
    mq_o[...] = p[:, shift:].astype(mq_o.dtype)

    row = lax.broadcasted_iota(jnp.int32, ps.shape, 0)
    prev = jnp.where(row == 0, carry_ref[...], pltpu.roll(ps, 1, 0))
    carry_ref[...] = ps[tm - 1:tm, :]
    z = ps + (prev - ps) * mu_ref[...]

    r = z[:, :c]
    kraw = z[:, c:2 * c]
    v = z[:, 2 * c:3 * c]
    lo = z[:, 3 * c:]
    lane = lax.broadcasted_iota(jnp.int32, lo.shape, 1)
    lhs = jnp.where(lane < DECAY_LORA, jnp.tanh(lo),
                    jnp.where(lane < DECAY_LORA + AAA_LORA, lo, jax.nn.sigmoid(lo)))
    pre = _dot(lhs.astype(BF16), lora_ref[...])
    w = -_softplus(-(pre[:, :c] + w0_ref[...])) - 0.5
    a = jax.nn.sigmoid(pre[:, c:2 * c] + a0_ref[...])
    g_o[...] = pre[:, 2 * c:]

    kkv = kraw * kkw_ref[...]
    ss = _head_sum(kkv * kkv, sel_ref, back_ref, 2)
    kkn = kkv * lax.rsqrt(jnp.maximum(ss, 1e-24))
    k2 = kraw * (1.0 + (a - 1.0) * kaw_ref[...])

    r_o[...] = r
    lw_o[...] = -jnp.exp(w)
    k_o[...] = k2
    v_o[...] = v
    kk_o[...] = kkn
    b_o[...] = kkn * a
    bonus_o[...] = _head_sum(r * k2 * rkw_ref[...], sel_ref, back_ref, 2) * v


def _rwkv_prep(x, w_in, mu, lora, w0, a0, k_k, k_a, r_k, sel, back, *, seq, tm):
    t, d = x.shape
    c = w0.shape[1]
    n_in = w_in.shape[1]
    shift = 3 * c + LORA_WIDTH
    row = lambda i: (i, 0)
    const = lambda i: (0, 0)
    wide = jax.ShapeDtypeStruct((t, c), F32)
    return pl.pallas_call(
        functools.partial(_rwkv_prep_kernel, tiles_per_seq=seq // tm, c=c),
        grid=(t // tm,),
        in_specs=[
            pl.BlockSpec((tm, d), row),
            pl.BlockSpec((d, n_in), const),
            pl.BlockSpec((1, shift), const),
            pl.BlockSpec((LORA_WIDTH, 3 * c), const),
            pl.BlockSpec((1, c), const), pl.BlockSpec((1, c), const), pl.BlockSpec((1, c), const),
            pl.BlockSpec((1, c), const), pl.BlockSpec((1, c), const),
            pl.BlockSpec(sel.shape, const), pl.BlockSpec(back.shape, const),
        ],
        out_specs=[pl.BlockSpec((tm, c), row)] * 8 + [pl.BlockSpec((tm, MEM_WIDTH), row)],
        out_shape=[wide] * 8 + [jax.ShapeDtypeStruct((t, MEM_WIDTH), BF16)],
        scratch_shapes=[pltpu.VMEM((1, shift), F32)],
        compiler_params=_cparams("arbitrary"),
        name="rwkv_prep",
    )(x, w_in, mu, lora, w0, a0, k_k, k_a, r_k, sel, back)


def _rwkv_chunk_kernel(r_ref, lw_ref, k_ref, v_ref, kk_ref, b_ref, o_ref, s_ref, *, n_pairs, n_chunks):
    @pl.when(pl.program_id(1) == 0)
    def _():
        s_ref[...] = jnp.zeros_like(s_ref)

    L = RWKV_CHUNK
    width = 2 * HEAD_DIM
    ti = lax.broadcasted_iota(jnp.int32, (L, width), 0)
    lane = lax.broadcasted_iota(jnp.int32, (L, width), 1)
    si = lane % HEAD_DIM
    first = lane < HEAD_DIM
    incl = si <= ti
    strict = si < ti
    eye = (si == ti).astype(F32)
    same_block = (si // INV_BLOCK) == (ti // INV_BLOCK)
    tri = lax.broadcasted_iota(jnp.int32, (L, L), 0) >= lax.broadcasted_iota(jnp.int32, (L, L), 1)
    tril_ones = tri.astype(BF16)

    def bdiag(x):
        x = x.astype(BF16)
        zero = jnp.zeros_like(x)
        return jnp.concatenate([jnp.where(first, x, zero), jnp.where(first, zero, x)], axis=0)

    def pdot(a, b):
        return _dot(a.astype(BF16), bdiag(b))

    items = []
    for ci in range(n_chunks):
        rows = slice(ci * L, (ci + 1) * L)
        lw = lw_ref[0, rows, :]
        cum = _split_dot_left(tril_ones, lw, 3)
        cum_last = cum[L - 1:L, :]
        g_inv = jnp.exp(-cum)
        g_rel = jnp.exp(cum_last - cum)
        kk = kk_ref[0, rows, :]
        b = b_ref[0, rows, :]
        k = k_ref[0, rows, :]
        kap = kk * jnp.exp(cum - lw)
        rho = r_ref[0, rows, :] * jnp.exp(cum)
        bet = b * g_inv
        kt = k * g_inv
        bg = b * g_rel
        kg = k * g_rel
        v = v_ref[0, rows, :]
        g_last = jnp.exp(cum_last)
        for p in range(n_pairs):
            sl = slice(p * width, (p + 1) * width)
            items.append(dict(kap=kap[:, sl], rho=rho[:, sl], bet=bet[:, sl], kt=kt[:, sl], bg=bg[:, sl],
                              kg=kg[:, sl], v=v[:, sl], g_last=g_last[:, sl]))
    prod = [_dot_nt(jnp.concatenate([it["kap"], it["rho"]], axis=0).astype(BF16),
                    jnp.concatenate([bdiag(it["bet"]), bdiag(it["kt"])], axis=0)) for it in items]
    m_ab = [jnp.where(strict, x[:L, :width], 0.0) for x in prod]
    m_ak = [jnp.where(strict, x[:L, width:], 0.0) for x in prod]
    a_r = [jnp.concatenate([jnp.where(incl, x[L:, :width], 0.0), jnp.where(incl, x[L:, width:], 0.0)],
                           axis=1).astype(BF16) for x in prod]

    md = [jnp.where(same_block, m, 0.0) for m in m_ab]
    nd = [m - d for m, d in zip(m_ab, md)]
    m2 = [pdot(d, d) for d in md]
    m4 = [pdot(x, x) for x in m2]
    m8 = [pdot(x, x) for x in m4]
    di = [eye - d for d in md]
    di = [x + pdot(x, q) for x, q in zip(di, m2)]
    di = [x + pdot(x, q) for x, q in zip(di, m4)]
    di = [x + pdot(x, q) for x, q in zip(di, m8)]
    xn = [pdot(x, y) for x, y in zip(di, nd)]
    x2 = [pdot(x, x) for x in xn]
    yi = [eye - x for x in xn]
    yi = [a + pdot(a, q) for a, q in zip(yi, x2)]
    t_inv = [pdot(a, x) for a, x in zip(yi, di)]

    bd_v = [bdiag(it["v"]) for it in items]
    mv = [_dot(m.astype(BF16), x) for m, x in zip(m_ak, bd_v)]
    wu = [_dot(t.astype(BF16), jnp.concatenate([bdiag(it["kap"]), bdiag(x)], axis=1))
          for t, it, x in zip(t_inv, items, mv)]

    state = [s_ref[p] for p in range(n_pairs)]
    for ci in range(n_chunks):
        idx = range(ci * n_pairs, (ci + 1) * n_pairs)
        ws = [_dot_nt(jnp.concatenate([wu[i][:, :width], items[i]["rho"]], axis=0).astype(BF16), bdiag(s))
              for i, s in zip(idx, state)]
        u = [-wu[i][:, width:] - x[:L] for i, x in zip(idx, ws)]
        for n, (i, x, y) in enumerate(zip(idx, ws, u)):
            o_ref[0, ci * L:(ci + 1) * L, n * width:(n + 1) * width] = (
                x[L:] + _dot(a_r[i], jnp.concatenate([bdiag(y), bd_v[i]], axis=0)))
        upd = [_dot_tn(jnp.concatenate([y, items[i]["v"]], axis=0).astype(BF16),
                       jnp.concatenate([items[i]["bg"], items[i]["kg"]], axis=0).astype(BF16))
               for i, y in zip(idx, u)]
        state = [s * items[i]["g_last"] + jnp.where(first, x[:L], x[L:]) for i, s, x in zip(idx, state, upd)]
    for p in range(n_pairs):
        s_ref[p] = state[p]


def _split_dot_left(ones, x, terms):
    acc = None
    rem = x
    for _ in range(terms):
        piece = rem.astype(BF16)
        part = _dot(ones, piece)
        acc = part if acc is None else acc + part
        rem = rem - piece.astype(F32)
    return acc


def _rwkv_chunk(r, lw, k, v, kk, b, *, n_chunks):
    bsz, seq, c = r.shape
    n_pairs = c // (2 * HEAD_DIM)
    rows = n_chunks * RWKV_CHUNK
    blk = pl.BlockSpec((1, rows, c), lambda bi, ci: (bi, ci, 0))
    return pl.pallas_call(
        functools.partial(_rwkv_chunk_kernel, n_pairs=n_pairs, n_chunks=n_chunks),
        grid=(bsz, seq // rows),
        in_specs=[blk] * 6,
        out_specs=blk,
        out_shape=jax.ShapeDtypeStruct((bsz, seq, c), F32),
        scratch_shapes=[pltpu.VMEM((n_pairs, HEAD_DIM, 2 * HEAD_DIM), F32)],
        compiler_params=_cparams("parallel", "arbitrary"),
        name="rwkv_chunk",
    )(r, lw, k, v, kk, b)


def _max_sq_norm(x):
    width = x.shape[1]
    chan = lax.broadcasted_iota(jnp.int32, (width, LANES), 0) // HEAD_DIM
    sel = (chan == lax.broadcasted_iota(jnp.int32, (width, LANES), 1)).astype(BF16)
    xf = x.astype(F32)
    return jnp.max(_dot((xf * xf).astype(BF16), sel), keepdims=True)


def _sb_proj_kernel(x_ref, w_ref, s_ref, o_ref, ksq_ref, *, c):
    p = (_dot(x_ref[...].astype(BF16), w_ref[...]) * s_ref[...]).astype(o_ref.dtype)
    o_ref[...] = p
    ksq_ref[...] = jnp.broadcast_to(_max_sq_norm(p[:, c:2 * c]), ksq_ref.shape)


def _sb_proj(x, w, col_scale, *, c, tm):
    t, d = x.shape
    n = w.shape[1]
    return pl.pallas_call(
        functools.partial(_sb_proj_kernel, c=c),
        grid=(t // tm,),
        in_specs=[pl.BlockSpec((tm, d), lambda i: (i, 0)), _resident((d, n)), _resident((1, n))],
        out_specs=[pl.BlockSpec((tm, n), lambda i: (i, 0)), pl.BlockSpec((SUBLANES, LANES), lambda i: (i, 0))],
        out_shape=[jax.ShapeDtypeStruct((t, n), BF16),
                   jax.ShapeDtypeStruct((t // tm * SUBLANES, LANES), F32)],
        compiler_params=_cparams("parallel"),
        name="sb_proj",
    )(x, w, col_scale)


def _sb_kernel(q_ref, k_ref, v_ref, ksq_ref, o_ref, *, n_heads):
    qi = pl.program_id(2)
    tq = q_ref.shape[0]
    ti = lax.broadcasted_iota(jnp.int32, (tq, tq), 0)
    si = lax.broadcasted_iota(jnp.int32, (tq, tq), 1)
    before = si < ti
    suffix = jnp.concatenate([(ti >= si).astype(BF16)] * SB_TERMS, axis=0)
    q_all = q_ref[...]
    n_groups = n_heads // SB_GROUP

    def key_block(j, carries, accs, diagonal):
        rows = pl.ds(pl.multiple_of(j * tq, tq), tq)
        kb = k_ref[rows, :]
        vb = v_ref[rows, :]
        ys, sums = {}, {}
        new_carries, new_accs = list(carries), list(accs)

        def scores(g):
            sps = []
            for h in range(g * SB_GROUP, (g + 1) * SB_GROUP):
                sl = slice(h * HEAD_DIM, (h + 1) * HEAD_DIM)
                y = _dot_nt(q_all[:, sl], kb[:, sl])
                sp = jnp.maximum(y, 0.0) + jnp.log2(1.0 + jnp.exp2(-jnp.abs(y)))
                if diagonal:
                    sp = jnp.where(before, sp, 0.0)
                ys[h] = y
                pieces = []
                for _ in range(SB_TERMS):
                    piece = sp.astype(BF16)
                    pieces.append(piece)
                    sp = sp - piece.astype(F32)
                sps.append(jnp.concatenate(pieces, axis=1))
            out = _dot(jnp.concatenate(sps, axis=0), suffix)
            for i, h in enumerate(range(g * SB_GROUP, (g + 1) * SB_GROUP)):
                sums[h] = out[i * tq:(i + 1) * tq, :]

        def weights(g):
            for h in range(g * SB_GROUP, (g + 1) * SB_GROUP):
                sl = slice(h * HEAD_DIM, (h + 1) * HEAD_DIM)
                att = jnp.exp2(ys[h] - sums[h] - carries[h])
                if diagonal:
                    att = jnp.where(before, att, 0.0)
                new_accs[h] = accs[h] + _dot(att.astype(BF16), vb[:, sl])
                new_carries[h] = carries[h] + sums[h][:, 0:1]

        for g in range(n_groups + 1):
            if g < n_groups:
                scores(g)
            if g > 0:
                weights(g - 1)
        return tuple(new_carries), tuple(new_accs)

    zeros_c = tuple(jnp.zeros((tq, 1), F32) for _ in range(n_heads))
    zeros_a = tuple(jnp.zeros((tq, HEAD_DIM), F32) for _ in range(n_heads))
    carries, accs = key_block(qi, zeros_c, zeros_a, True)

    bound = jnp.sqrt(_max_sq_norm(q_all) * ksq_ref[0, 0:1, 0:1]) * SB_BOUND_MARGIN

    def alive(state):
        jj, carries, _ = state
        slack = functools.reduce(jnp.minimum, [jnp.min(c, keepdims=True) for c in carries]) - bound
        return jnp.logical_and(jj < qi, slack[0, 0] < SB_DEAD_LOG2)

    def body(state):
        jj, carries, accs = state
        carries, accs = key_block(qi - 1 - jj, carries, accs, False)
        return jj + 1, carries, accs

    _, _, accs = lax.while_loop(alive, body, (jnp.int32(0), carries, accs))
    o_ref[...] = jnp.concatenate(accs, axis=1).astype(o_ref.dtype)


def _sb_attention(p, ksq, *, bsz, seq, c):
    pair = SB_HEADS * HEAD_DIM
    n_pairs = c // pair
    nq = seq // SB_BLOCK
    return pl.pallas_call(
        functools.partial(_sb_kernel, n_heads=SB_HEADS),
        grid=(bsz, n_pairs, nq),
        in_specs=[
            pl.BlockSpec((SB_BLOCK, pair), lambda b, h, i: (b * nq + i, h)),
            pl.BlockSpec((seq, pair), lambda b, h, i: (b, n_pairs + h)),
            pl.BlockSpec((seq, pair), lambda b, h, i: (b, 2 * n_pairs + h)),
            pl.BlockSpec((1, SUBLANES, LANES), lambda b, h, i: (b, 0, 0)),
        ],
        out_specs=pl.BlockSpec((SB_BLOCK, pair), lambda b, h, i: (b * nq + i, h)),
        out_shape=jax.ShapeDtypeStruct((bsz * seq, c), BF16),
        compiler_params=_cparams("parallel", "parallel", "arbitrary"),
        name="sb_attention",
    )(p, p, p, ksq)


def _memory_attention(mq, mem_kv):
    scale = 1.0 / math.sqrt(HEAD_DIM)
    tm = mq.shape[0]
    items = [(r, h) for r in range(0, tm, MEM_ROWS) for h in range(N_MEM_HEADS)]
    s = [_dot_nt(mq[r:r + MEM_ROWS, h * HEAD_DIM:(h + 1) * HEAD_DIM], mem_kv[:, h * HEAD_DIM:(h + 1) * HEAD_DIM])
         * scale for r, h in items]
    e = [jnp.exp(x - jnp.max(x, axis=-1, keepdims=True)).astype(BF16) for x in s]
    ones = jnp.ones((mem_kv.shape[0], HEAD_DIM), BF16)
    den = [_dot(x, ones) for x in e]
    o = [_dot(x, mem_kv[:, MEM_WIDTH + h * HEAD_DIM:MEM_WIDTH + (h + 1) * HEAD_DIM]) / d
         for x, d, (r, h) in zip(e, den, items)]
    chunks = [jnp.concatenate(o[i:i + N_MEM_HEADS], axis=1) for i in range(0, len(o), N_MEM_HEADS)]
    return jnp.concatenate(chunks, axis=0)


def _out_ln(mix_bf16, mq_ref, memkv_ref, x_ref, wout_ref, g_ref, b_ref, o_ref, *, alpha):
    c = mix_bf16.shape[1]
    mem_o = _memory_attention(mq_ref[...], memkv_ref[0])
    y = _dot(mix_bf16, wout_ref[:c, :]) + _dot(mem_o.astype(BF16), wout_ref[c:, :])
    o_ref[...] = _layer_norm(alpha * x_ref[...] + y, g_ref[...], b_ref[...])


def _sb_out_kernel(mix_ref, mq_ref, memkv_ref, x_ref, wout_ref, g_ref, b_ref, o_ref, *, alpha):
    _out_ln(mix_ref[...], mq_ref, memkv_ref, x_ref, wout_ref, g_ref, b_ref, o_ref, alpha=alpha)


def _rwkv_out_kernel(raw_ref, bonus_ref, gate_ref, lg_ref, lb_ref, sel_ref, back_ref,
                     mq_ref, memkv_ref, x_ref, wout_ref, g_ref, b_ref, o_ref, *, alpha):
    inv_n = 1.0 / HEAD_DIM
    mixes = []
    for r in range(0, raw_ref.shape[0], MEM_ROWS):
        rows = slice(r, r + MEM_ROWS)
        raw = raw_ref[rows, :]
        m = _head_sum(raw, sel_ref, back_ref, 3) * inv_n
        d = raw - m
        var = _head_sum(d * d, sel_ref, back_ref, 2) * inv_n
        normed = d * lax.rsqrt(var + LNX_EPS) * lg_ref[...] + lb_ref[...]
        mixes.append(((normed + bonus_ref[rows, :]) * gate_ref[rows, :]).astype(BF16))
    _out_ln(jnp.concatenate(mixes, axis=0), mq_ref, memkv_ref, x_ref, wout_ref, g_ref, b_ref, o_ref, alpha=alpha)


def _mix_out_ln(mix_inputs, mq, mq_col, mem_kv, x, w_out, g, b, *, alpha, seq, tm, rwkv):
    t, d = x.shape
    tiles_per_seq = seq // tm
    row = lambda i: (i, 0)
    const = lambda i: (0, 0)
    c = d - MEM_WIDTH
    if rwkv:
        kern = _rwkv_out_kernel
        mix_specs = ([pl.BlockSpec((tm, c), row)] * 3 + [pl.BlockSpec((1, c), const)] * 2
                     + [pl.BlockSpec(a.shape, const) for a in mix_inputs[5:]])
    else:
        kern = _sb_out_kernel
        mix_specs = [pl.BlockSpec((tm, c), row)]
    return pl.pallas_call(
        functools.partial(kern, alpha=alpha),
        grid=(t // tm,),
        in_specs=mix_specs + [
            pl.BlockSpec((tm, MEM_WIDTH), lambda i: (i, mq_col)),
            pl.BlockSpec((1,) + mem_kv.shape[1:], lambda i: (i // tiles_per_seq, 0, 0)),
            pl.BlockSpec((tm, d), row),
            pl.BlockSpec((d, d), const),
            pl.BlockSpec((1, d), const),
            pl.BlockSpec((1, d), const),
        ],
        out_specs=pl.BlockSpec((tm, d), row),
        out_shape=jax.ShapeDtypeStruct((t, d), F32),
        compiler_params=_cparams("parallel"),
        name="rwkv_out_ln" if rwkv else "sb_out_ln",
    )(*mix_inputs, mq, mem_kv, x, w_out, g, b)


def kernel(x, mem, ffn1_w_gate, ffn1_w_up, ffn1_w_down, ffn2_w_gate, ffn2_w_up, ffn2_w_down, ln_g, ln_b, w_out, w_mem_kv, rwkv_w_in, rwkv_mu, rwkv_w0, rwkv_w_up, rwkv_a0, rwkv_a_up, rwkv_g_up, rwkv_k_k, rwkv_k_a, rwkv_r_k, rwkv_lnx_g, rwkv_lnx_b, sb_w_in):
    bsz, seq, d = x.shape
    depth = ln_g.shape[0]
    c = d - MEM_WIDTH
    alpha = (2 * depth) ** 0.25
    n_mem = mem.shape[1]
    tm = min(512, seq)
    tf = 256

    head_of = jnp.arange(c) // HEAD_DIM
    sel = (head_of[:, None] == jnp.arange(LANES)[None, :]).astype(BF16)
    back = jnp.tile(sel.T, (HEAD_SUM_TERMS, 1))

    mem_kv = _proj(mem.reshape(bsz * n_mem, d), w_mem_kv.astype(BF16), jnp.ones((1, 2 * MEM_WIDTH), F32),
                   tm=256, out_dtype=BF16)
    mem_kv = mem_kv.reshape(bsz, n_mem, 2 * MEM_WIDTH)

    ffn1 = tuple(w.astype(BF16) for w in (ffn1_w_gate, ffn1_w_up, ffn1_w_down))
    ffn2 = tuple(w.astype(BF16) for w in (ffn2_w_gate, ffn2_w_up, ffn2_w_down))

    def ffn(xf, weights, layer, g, b):
        return _ffn_ln(xf, *weights, g[None], b[None], layer, alpha=alpha, tm=tm, tf=tf)

    xf = x.reshape(bsz * seq, d)
    for i in range(depth):
        xf = ffn(xf, ffn1, i, ln_g[i, 0], ln_b[i, 0])
        j = i // 2
        wo = w_out[i].astype(BF16)
        if i % 2 == 0:
            lora = jnp.zeros((LORA_WIDTH, 3 * c), F32)
            lora = lora.at[:DECAY_LORA, :c].set(rwkv_w_up[j])
            lora = lora.at[DECAY_LORA:DECAY_LORA + AAA_LORA, c:2 * c].set(rwkv_a_up[j])
            lora = lora.at[DECAY_LORA + AAA_LORA:, 2 * c:].set(rwkv_g_up[j])
            r, lw, k, v, kk, b, gate, bonus, mq = _rwkv_prep(
                xf, rwkv_w_in[j].astype(BF16), rwkv_mu[j][None], lora.astype(BF16),
                rwkv_w0[j][None], rwkv_a0[j][None], rwkv_k_k[j][None], rwkv_k_a[j][None],
                rwkv_r_k[j].reshape(1, c), sel, back, seq=seq, tm=256)
            sh = (bsz, seq, c)
            raw = _rwkv_chunk(r.reshape(sh), lw.reshape(sh), k.reshape(sh), v.reshape(sh),
                              kk.reshape(sh), b.reshape(sh), n_chunks=RWKV_CHUNKS_PER_STEP).reshape(bsz * seq, c)
            xf = _mix_out_ln((raw, bonus, gate, rwkv_lnx_g[j][None], rwkv_lnx_b[j][None], sel, back), mq, 0, mem_kv,
                             xf, wo, ln_g[i, 1][None], ln_b[i, 1][None], alpha=alpha, seq=seq, tm=tm, rwkv=True)
        else:
            q_scale = math.log2(math.e) / math.sqrt(HEAD_DIM)
            col_scale = jnp.where(jnp.arange(3 * c + MEM_WIDTH) < c, q_scale, 1.0).astype(F32)[None]
            p, ksq = _sb_proj(xf, sb_w_in[j].astype(BF16), col_scale, c=c, tm=tm)
            ksq = jnp.max(ksq.reshape(bsz, -1), axis=1)
            ksq = jnp.broadcast_to(ksq[:, None, None], (bsz, SUBLANES, LANES))
            mix = _sb_attention(p, ksq, bsz=bsz, seq=seq, c=c)
            xf = _mix_out_ln((mix,), p, (3 * c) // MEM_WIDTH, mem_kv, xf, wo, ln_g[i, 1][None], ln_b[i, 1][None],
                             alpha=alpha, seq=seq, tm=tm, rwkv=False)
        xf = ffn(xf, ffn2, i, ln_g[i, 2], ln_b[i, 2])
    return xf.reshape(bsz, seq, d)
```
